```python
import math
import jax, jax.numpy as jnp
from jax import lax
import numpy as np

D_MODEL = 1024
BATCH = 16
SEQ = 2048
DEPTH = 1

N_SUBLAYERS = 3
N_MOD = 3
D_FF = 2816
POOL_WINDOWS = (2, 4, 8, 16)
POOL_GROUPS = len(POOL_WINDOWS)
POOL_WIDTH = D_MODEL // 2
POOL_GROUP_DIM = POOL_WIDTH // POOL_GROUPS
N_HEADS = 8
QK_NOPE_DIM = 64
QK_ROPE_DIM = 32
V_HEAD_DIM = 64
QK_HEAD_DIM = QK_NOPE_DIM + QK_ROPE_DIM
Q_LORA_RANK = 384
KV_LORA_RANK = 256
MLA_WIDTH = N_HEADS * V_HEAD_DIM
ROPE_THETA = 10000.0
Q_BLOCK = 128
ATTN_SCALE = 1.0 / math.sqrt(QK_HEAD_DIM)
NORM_EPS = 1e-6
IN_SPLITS = (POOL_WIDTH, Q_LORA_RANK, KV_LORA_RANK, QK_ROPE_DIM, D_MODEL, D_MODEL)
IN_WIDTH = sum(IN_SPLITS)

kernel_name = "hybrid_pool_mla_macaron_adaln"


def rms_norm(x, g):
    xf = x.astype(jnp.float32)
    y = xf * lax.rsqrt(jnp.mean(xf * xf, axis=-1, keepdims=True) + NORM_EPS)
    return (y * g.astype(jnp.float32)).astype(x.dtype)


def modulate(h, shift, scale):
    return h * (1.0 + scale[:, None, :]) + shift[:, None, :]


def swiglu(h, w_in, w_out):
    gu = h @ w_in
    g, u = jnp.split(gu, 2, axis=-1)
    return (jax.nn.silu(g) * u) @ w_out


def rope_tables(positions):
    inv_freq = ROPE_THETA ** (-jnp.arange(0, QK_ROPE_DIM, 2, dtype=jnp.float32) / QK_ROPE_DIM)
    ang = positions.astype(jnp.float32)[..., None] * inv_freq
    ang = jnp.concatenate([ang, ang], axis=-1)
    return jnp.cos(ang), jnp.sin(ang)


def apply_rope(x, cos, sin):
    xf = x.astype(jnp.float32)
    x1, x2 = jnp.split(xf, 2, axis=-1)
    rot = jnp.concatenate([-x2, x1], axis=-1)
    return (xf * cos + rot * sin).astype(x.dtype)


def causal_multiscale_pool(u):
    B, S, _ = u.shape
    uf = u.astype(jnp.float32).reshape(B, S, POOL_GROUPS, POOL_GROUP_DIM)
    cs = jnp.pad(jnp.cumsum(uf, axis=1), ((0, 0), (1, 0), (0, 0), (0, 0)))
    t = jnp.arange(S)
    outs = []
    for g, w in enumerate(POOL_WINDOWS):
        hi = cs[:, 1:, g]
        lo = cs[:, jnp.maximum(t + 1 - w, 0), g]
        cnt = jnp.minimum(t + 1, w).astype(jnp.float32)
        outs.append((hi - lo) / cnt[None, :, None])
    pooled = jnp.stack(outs, axis=2)
    return (pooled - uf).astype(u.dtype)


def causal_mla_attention(q_nope, q_rope, k_nope, k_rope, v):
    S = q_nope.shape[1]
    outs = []
    for i in range(S // Q_BLOCK):
        q0, q1 = i * Q_BLOCK, (i + 1) * Q_BLOCK
        s = (jnp.einsum('bqhd,bkhd->bhqk', q_nope[:, q0:q1], k_nope[:, :q1])
             + jnp.einsum('bqhr,bkr->bhqk', q_rope[:, q0:q1], k_rope[:, :q1]))
        s = s.astype(jnp.float32) * ATTN_SCALE
        mask = jnp.arange(q1)[None, :] <= (q0 + jnp.arange(Q_BLOCK))[:, None]
        s = jnp.where(mask, s, jnp.float32(-1e30))
        p = jax.nn.softmax(s, axis=-1).astype(v.dtype)
        outs.append(jnp.einsum('bhqk,bkhd->bqhd', p, v[:, :q1]))
    return jnp.concatenate(outs, axis=1)


def setup_inputs(seed: int = 0) -> dict:
    key = jax.random.key(seed)
    ks = jax.random.split(key, 32)
    f32 = jnp.float32

    def w(k, shape, fan_in, gain=1.0):
        return jax.random.normal(k, shape, f32) * (gain * fan_in ** -0.5)

    def gain(k, shape):
        return jnp.ones(shape, f32) + 0.05 * jax.random.normal(k, shape, f32)

    L = DEPTH
    x = jax.random.normal(ks[0], (BATCH, SEQ, D_MODEL), f32)
    c = jax.random.normal(ks[1], (BATCH, D_MODEL), f32)
    offs = jax.random.randint(ks[2], (BATCH, 1), 0, 256, dtype=jnp.int32)
    positions = jnp.arange(SEQ, dtype=jnp.int32)[None, :] + offs
    return {
        "x": x,
        "c": c,
        "positions": positions,
        "w_ada": w(ks[3], (L, D_MODEL, N_SUBLAYERS * N_MOD * D_MODEL), D_MODEL, 0.5),
        "b_ada": 0.05 * jax.random.normal(ks[4], (L, N_SUBLAYERS * N_MOD * D_MODEL), f32),
        "norm_ffn1": gain(ks[5], (L, D_MODEL)),
        "w_ffn1_in": w(ks[6], (L, D_MODEL, 2 * D_FF), D_MODEL),
        "w_ffn1_out": w(ks[7], (L, D_FF, D_MODEL), D_FF),
        "norm_mix": gain(ks[8], (L, D_MODEL)),
        "w_in": w(ks[9], (L, D_MODEL, IN_WIDTH), D_MODEL),
        "pool_grp": w(ks[10], (L, POOL_GROUPS, POOL_GROUP_DIM, POOL_GROUP_DIM), POOL_GROUP_DIM),
        "pool_scale": gain(ks[11], (L, POOL_WIDTH)),
        "w_pool_proj": w(ks[12], (L, POOL_WIDTH, D_MODEL), POOL_WIDTH),
        "q_a_norm": gain(ks[13], (L, Q_LORA_RANK)),
        "w_q_up": w(ks[14], (L, Q_LORA_RANK, N_HEADS * QK_HEAD_DIM), Q_LORA_RANK),
        "kv_a_norm": gain(ks[15], (L, KV_LORA_RANK)),
        "w_kv_up": w(ks[16], (L, KV_LORA_RANK, N_HEADS * (QK_NOPE_DIM + V_HEAD_DIM)), KV_LORA_RANK),
        "q_norm_nope": gain(ks[17], (L, QK_NOPE_DIM)),
        "q_norm_rope": gain(ks[18], (L, QK_ROPE_DIM)),
        "k_norm_nope": gain(ks[19], (L, QK_NOPE_DIM)),
        "k_norm_rope": gain(ks[20], (L, QK_ROPE_DIM)),
        "w_mla_proj": w(ks[21], (L, MLA_WIDTH, D_MODEL), MLA_WIDTH),
        "w_out": w(ks[22], (L, D_MODEL, D_MODEL), D_MODEL),
        "norm_ffn2": gain(ks[23], (L, D_MODEL)),
        "w_ffn2_in": w(ks[24], (L, D_MODEL, 2 * D_FF), D_MODEL),
        "w_ffn2_out": w(ks[25], (L, D_FF, D_MODEL), D_FF),
    }


def reference(x, c, positions, w_ada, b_ada, norm_ffn1, w_ffn1_in, w_ffn1_out,
              norm_mix, w_in, pool_grp, pool_scale, w_pool_proj, q_a_norm, w_q_up,
              kv_a_norm, w_kv_up, q_norm_nope, q_norm_rope, k_norm_nope, k_norm_rope,
              w_mla_proj, w_out, norm_ffn2, w_ffn2_in, w_ffn2_out):
    B, S, D = x.shape
    cos, sin = rope_tables(positions)
    c_act = jax.nn.silu(c)
    split_pts = list(np.cumsum(IN_SPLITS)[:-1])

    for l in range(DEPTH):
        mod = (c_act @ w_ada[l] + b_ada[l]).reshape(B, N_SUBLAYERS, N_MOD, D)

        h = modulate(rms_norm(x, norm_ffn1[l]), mod[:, 0, 0], mod[:, 0, 1])
        x = x + 0.5 * mod[:, 0, 2][:, None, :] * swiglu(h, w_ffn1_in[l], w_ffn1_out[l])

        h = modulate(rms_norm(x, norm_mix[l]), mod[:, 1, 0], mod[:, 1, 1])
        u_pool, q_lat, kv_lat, k_rope, g_pool, g_mla = jnp.split(h @ w_in[l], split_pts, axis=-1)

        pooled = causal_multiscale_pool(u_pool).reshape(B, S, POOL_GROUPS, POOL_GROUP_DIM)
        pooled = jnp.einsum('bsgc,gcd->bsgd', pooled, pool_grp[l]).reshape(B, S, POOL_WIDTH)
        br_pool = (pooled * pool_scale[l]) @ w_pool_proj[l]

        q = (rms_norm(q_lat, q_a_norm[l]) @ w_q_up[l]).reshape(B, S, N_HEADS, QK_HEAD_DIM)
        q_nope, q_rope = q[..., :QK_NOPE_DIM], q[..., QK_NOPE_DIM:]
        kv = (rms_norm(kv_lat, kv_a_norm[l]) @ w_kv_up[l]).reshape(
            B, S, N_HEADS, QK_NOPE_DIM + V_HEAD_DIM)
        k_nope, v = kv[..., :QK_NOPE_DIM], kv[..., QK_NOPE_DIM:]
        q_nope = rms_norm(q_nope, q_norm_nope[l])
        k_nope = rms_norm(k_nope, k_norm_nope[l])
        q_rope = apply_rope(rms_norm(q_rope, q_norm_rope[l]), cos[:, :, None, :], sin[:, :, None, :])
        k_rope = apply_rope(rms_norm(k_rope, k_norm_rope[l]), cos, sin)
        attn = causal_mla_attention(q_nope, q_rope, k_nope, k_rope, v).reshape(B, S, MLA_WIDTH)
        br_mla = attn @ w_mla_proj[l]

        merged = jax.nn.sigmoid(g_pool) * br_pool + jax.nn.sigmoid(g_mla) * br_mla
        x = x + mod[:, 1, 2][:, None, :] * (merged @ w_out[l])

        h = modulate(rms_norm(x, norm_ffn2[l]), mod[:, 2, 0], mod[:, 2, 1])
        x = x + 0.5 * mod[:, 2, 2][:, None, :] * swiglu(h, w_ffn2_in[l], w_ffn2_out[l])

    return x
```

```python
import functools
import math

import numpy as np
import jax
import jax.numpy as jnp
from jax import lax
from jax.experimental import pallas as pl
from jax.experimental.pallas import tpu as pltpu

F32 = jnp.float32
BF16 = jnp.bfloat16

D_MODEL = 1024
D_FF = 2816
N_MOD_ROWS = 9
POOL_WINDOWS = (2, 4, 8, 16)
POOL_WIDTH = 512
POOL_GROUP_DIM = 128
POOL_HALO = 16
N_HEADS = 8
QK_NOPE_DIM = 64
QK_ROPE_DIM = 32
V_HEAD_DIM = 64
HEAD_PAD = 128
Q_LORA_RANK = 384
KV_LORA_RANK = 256
MLA_WIDTH = N_HEADS * V_HEAD_DIM
ROPE_THETA = 10000.0
ATTN_SCALE = 1.0 / math.sqrt(QK_NOPE_DIM + QK_ROPE_DIM)
NORM_EPS = 1e-6
MASK_VALUE = -1e30

OFF_U = 0
OFF_QK = POOL_WIDTH
OFF_KV = OFF_QK + Q_LORA_RANK + HEAD_PAD
OFF_GP = OFF_KV + KV_LORA_RANK
OFF_GM = OFF_GP + D_MODEL
IN_WIDTH_PAD = OFF_GM + D_MODEL

TM_FFN = 512
TM_MIX = 512
TQ = 256
TK = 256
FF_CHUNK = 256
ADA_BLOCK = 1152
VMEM_LIMIT = 56 * 1024 * 1024


def _dot(a, b):
    return jnp.dot(a, b, preferred_element_type=F32)


def _rms_norm(x, g):
    return x * lax.rsqrt(jnp.mean(x * x, axis=-1, keepdims=True) + NORM_EPS) * g


def _norm_mod(x, g, shift, scale):
    return _rms_norm(x, g) * (1.0 + scale) + shift


def _ada_kernel(c_ref, w_ref, b_ref, o_ref):
    c = c_ref[...]
    c_act = (c * jax.nn.sigmoid(c)).astype(BF16)
    o_ref[...] = _dot(c_act, w_ref[...].astype(BF16)) + b_ref[...]


def _ada(c, w_ada, b_ada):
    batch = c.shape[0]
    n = w_ada.shape[1]
    return pl.pallas_call(
        _ada_kernel,
        grid=(n // ADA_BLOCK,),
        in_specs=[
            pl.BlockSpec((batch, D_MODEL), lambda i: (0, 0)),
            pl.BlockSpec((D_MODEL, ADA_BLOCK), lambda i: (0, i)),
            pl.BlockSpec((1, ADA_BLOCK), lambda i: (0, i)),
        ],
        out_specs=pl.BlockSpec((batch, ADA_BLOCK), lambda i: (0, i)),
        out_shape=jax.ShapeDtypeStruct((batch, n), F32),
        compiler_params=pltpu.CompilerParams(
            dimension_semantics=("arbitrary",), vmem_limit_bytes=VMEM_LIMIT),
        name="ada",
    )(c, w_ada, b_ada.reshape(1, n))


def _ffn_kernel(x_ref, mod_ref, g_ref, win_ref, wout_ref, o_ref, act_ref, *, sub):
    x = x_ref[...]
    mod = mod_ref[...]
    shift, scale, gate = mod[3 * sub:3 * sub + 1], mod[3 * sub + 1:3 * sub + 2], mod[3 * sub + 2:3 * sub + 3]
    h = _norm_mod(x, g_ref[...], shift, scale).astype(BF16)
    for c in range(D_FF // FF_CHUNK):
        lo = c * FF_CHUNK
        g = _dot(h, win_ref[:, lo:lo + FF_CHUNK])
        u = _dot(h, win_ref[:, D_FF + lo:D_FF + lo + FF_CHUNK])
        act_ref[:, lo:lo + FF_CHUNK] = (g * jax.nn.sigmoid(g) * u).astype(BF16)
    y = _dot(act_ref[...], wout_ref[...])
    o_ref[...] = x + (0.5 * gate) * y


def _const_spec(shape):
    return pl.BlockSpec(shape, lambda *_: (0,) * len(shape), pipeline_mode=pl.Buffered(1))


def _ffn(x2d, mod, g, w_in, w_out, *, sub, seq):
    tokens = x2d.shape[0]
    tiles_per_seq = seq // TM_FFN
    return pl.pallas_call(
        functools.partial(_ffn_kernel, sub=sub),
        grid=(tokens // TM_FFN,),
        in_specs=[
            pl.BlockSpec((TM_FFN, D_MODEL), lambda i: (i, 0)),
            pl.BlockSpec((pl.Squeezed(), N_MOD_ROWS, D_MODEL), lambda i: (i // tiles_per_seq, 0, 0)),
            _const_spec((1, D_MODEL)),
            _const_spec((D_MODEL, 2 * D_FF)),
            _const_spec((D_FF, D_MODEL)),
        ],
        out_specs=pl.BlockSpec((TM_FFN, D_MODEL), lambda i: (i, 0)),
        out_shape=jax.ShapeDtypeStruct((tokens, D_MODEL), F32),
        scratch_shapes=[pltpu.VMEM((TM_FFN, D_FF), BF16)],
        compiler_params=pltpu.CompilerParams(
            dimension_semantics=("arbitrary",), vmem_limit_bytes=VMEM_LIMIT),
        name=f"ffn{sub}",
    )(x2d, mod, g, w_in, w_out)


def _segment_mean_sq(z, seg):
    sq = z * z
    hi = sq.astype(BF16)
    lo = (sq - hi.astype(F32)).astype(BF16)
    width = seg.shape[0]
    outs = []
    for p in range(z.shape[1] // width):
        sl = slice(p * width, (p + 1) * width)
        outs.append(_dot(hi[:, sl], seg) + _dot(lo[:, sl], seg))
    return jnp.concatenate(outs, axis=1)


def _rope_head(z, cos_t, sin_lo, sin_hi):
    up = pltpu.roll(z, HEAD_PAD - QK_ROPE_DIM // 2, 1)
    down = pltpu.roll(z, QK_ROPE_DIM // 2, 1)
    return z * cos_t + up * sin_lo + down * sin_hi


def _mix_in_kernel(x_ref, mod_ref, pos_ref, gmix_ref, win_ref, pgrp_ref, pscale_ref, wpp_ref,
                   qan_ref, wq_ref, kvan_ref, wkv_ref, qg_ref, kg_ref, krg_ref, seg_ref, freq_ref,
                   q_out, k_out, v_out, p_out, g_out, ext_ref, *, tiles_per_seq):
    tm = x_ref.shape[0]
    tile_in_seq = pl.program_id(0) % tiles_per_seq
    x = x_ref[...]
    mod = mod_ref[...]
    h = _norm_mod(x, gmix_ref[...], mod[3:4], mod[4:5]).astype(BF16)

    u = _dot(h, win_ref[:, OFF_U:OFF_U + POOL_WIDTH])

    @pl.when(tile_in_seq == 0)
    def _():
        ext_ref[0:POOL_HALO, :] = jnp.zeros((POOL_HALO, POOL_WIDTH), F32)

    @pl.when(tile_in_seq != 0)
    def _():
        ext_ref[0:POOL_HALO, :] = ext_ref[tm:tm + POOL_HALO, :]

    ext_ref[POOL_HALO:POOL_HALO + tm, :] = u
    t_in_seq = tile_in_seq * tm + lax.broadcasted_iota(jnp.int32, (tm, 1), 0)
    pooled = []
    for grp, window in enumerate(POOL_WINDOWS):
        lanes = slice(grp * POOL_GROUP_DIM, (grp + 1) * POOL_GROUP_DIM)
        u_g = u[:, lanes]
        acc = u_g
        for back in range(1, window):
            acc = acc + ext_ref[POOL_HALO - back:POOL_HALO - back + tm, lanes]
        cnt = jnp.minimum(t_in_seq + 1, window).astype(F32)
        pooled.append(_dot((acc / cnt - u_g).astype(BF16), pgrp_ref[grp]))
    pooled = jnp.concatenate(pooled, axis=1) * pscale_ref[...]
    br_pool = _dot(pooled.astype(BF16), wpp_ref[...])
    g_pool = _dot(h, win_ref[:, OFF_GP:OFF_GP + D_MODEL])
    p_out[...] = jax.nn.sigmoid(g_pool) * br_pool
    g_mla = _dot(h, win_ref[:, OFF_GM:OFF_GM + D_MODEL])
    g_out[...] = jax.nn.sigmoid(g_mla)

    seg = seg_ref[...]
    lane = lax.broadcasted_iota(jnp.int32, (1, HEAD_PAD), 1)
    ang = pos_ref[...].astype(F32) * freq_ref[...]
    cos_a, sin_a = jnp.cos(ang), jnp.sin(ang)
    half = QK_ROPE_DIM // 2
    cos_t = jnp.where(lane < QK_NOPE_DIM, 1.0, cos_a)
    sin_lo = jnp.where((lane >= QK_NOPE_DIM) & (lane < QK_NOPE_DIM + half), -sin_a, 0.0)
    sin_hi = jnp.where((lane >= QK_NOPE_DIM + half) & (lane < QK_NOPE_DIM + 2 * half), sin_a, 0.0)

    qk = _dot(h, win_ref[:, OFF_QK:OFF_QK + Q_LORA_RANK + HEAD_PAD])
    q_lat, k_rope = qk[:, :Q_LORA_RANK], qk[:, Q_LORA_RANK:]
    q = _dot(_rms_norm(q_lat, qan_ref[...]).astype(BF16), wq_ref[...])
    qn = q * lax.rsqrt(_segment_mean_sq(q, seg) + NORM_EPS) * qg_ref[...]
    for hd in range(N_HEADS):
        lanes = slice(hd * HEAD_PAD, (hd + 1) * HEAD_PAD)
        q_out[:, lanes] = (_rope_head(qn[:, lanes], cos_t, sin_lo, sin_hi) * ATTN_SCALE).astype(BF16)

    kv_lat = _dot(h, win_ref[:, OFF_KV:OFF_KV + KV_LORA_RANK])
    kvn = _rms_norm(kv_lat, kvan_ref[...]).astype(BF16)
    k_nope = _dot(kvn, wkv_ref[:, 0:N_HEADS * HEAD_PAD])
    v_out[...] = _dot(kvn, wkv_ref[:, N_HEADS * HEAD_PAD:]).astype(BF16)
    kn = k_nope * lax.rsqrt(_segment_mean_sq(k_nope, seg) + NORM_EPS) * kg_ref[...]
    kr_ms = jnp.sum(k_rope * k_rope, axis=-1, keepdims=True) * (1.0 / QK_ROPE_DIM)
    kr = _rope_head(k_rope * lax.rsqrt(kr_ms + NORM_EPS) * krg_ref[...], cos_t, sin_lo, sin_hi)
    for hd in range(N_HEADS):
        lanes = slice(hd * HEAD_PAD, (hd + 1) * HEAD_PAD)
        k_out[:, lanes] = (kn[:, lanes] + kr).astype(BF16)


def _mix_in(x2d, mod, pos2d, gmix, w, *, seq):
    tokens = x2d.shape[0]
    tm = TM_MIX
    tiles_per_seq = seq // tm
    tok_spec = lambda width: pl.BlockSpec((tm, width), lambda i: (i, 0))
    consts = [gmix, w["w_in"], w["pool_grp"], w["pool_scale"], w["w_pool_proj"], w["q_a_norm"], w["w_q"],
              w["kv_a_norm"], w["w_kv"], w["q_gain"], w["k_gain"], w["kr_gain"], w["seg"], w["freq"]]
    return pl.pallas_call(
        functools.partial(_mix_in_kernel, tiles_per_seq=tiles_per_seq),
        grid=(tokens // tm,),
        in_specs=[
            tok_spec(D_MODEL),
            pl.BlockSpec((pl.Squeezed(), N_MOD_ROWS, D_MODEL), lambda i: (i // tiles_per_seq, 0, 0)),
            tok_spec(1),
        ] + [_const_spec(a.shape) for a in consts],
        out_specs=[tok_spec(N_HEADS * HEAD_PAD), tok_spec(N_HEADS * HEAD_PAD), tok_spec(MLA_WIDTH),
                   tok_spec(D_MODEL), tok_spec(D_MODEL)],
        out_shape=[
            jax.ShapeDtypeStruct((tokens, N_HEADS * HEAD_PAD), BF16),
            jax.ShapeDtypeStruct((tokens, N_HEADS * HEAD_PAD), BF16),
            jax.ShapeDtypeStruct((tokens, MLA_WIDTH), BF16),
            jax.ShapeDtypeStruct((tokens, D_MODEL), F32),
            jax.ShapeDtypeStruct((tokens, D_MODEL), F32),
        ],
        scratch_shapes=[pltpu.VMEM((POOL_HALO + tm, POOL_WIDTH), F32)],
        compiler_params=pltpu.CompilerParams(
            dimension_semantics=("arbitrary",), vmem_limit_bytes=VMEM_LIMIT),
        name="mix_in",
    )(x2d, mod, pos2d, *consts)


def _attn_kernel(q_ref, k_ref, v_ref, p_ref, g_ref, x_ref, mod_ref, wmla_ref, wout_ref, o_ref,
                 s_ref, attn_ref):
    j = pl.program_id(1)
    row = lax.broadcasted_iota(jnp.int32, (TQ, TK), 0)
    col = lax.broadcasted_iota(jnp.int32, (TQ, TK), 1)
    causal = col <= row
    lane = lax.broadcasted_iota(jnp.int32, (1, 2 * V_HEAD_DIM), 1)

    def head(hd):
        q_h = q_ref[:, hd * HEAD_PAD:(hd + 1) * HEAD_PAD]
        pair = hd // 2

        def scores(c):
            k_c = k_ref[pl.ds(pl.multiple_of(c * TK, TK), TK), hd * HEAD_PAD:(hd + 1) * HEAD_PAD]
            return lax.dot_general(q_h, k_c, (((1,), (1,)), ((), ())), preferred_element_type=F32)

        def lane_fold(s, op):
            out = s[:, 0:128]
            for t in range(1, TK // 128):
                out = op(out, s[:, t * 128:(t + 1) * 128])
            return out

        def pass1(c, m_run):
            s = scores(c)
            s_ref[c] = s
            return jnp.maximum(m_run, lane_fold(s, jnp.maximum))

        m_run = lax.fori_loop(0, j, pass1, jnp.full((TQ, 128), MASK_VALUE, F32))
        s = jnp.where(causal, scores(j), MASK_VALUE)
        s_ref[j] = s
        m_run = jnp.maximum(m_run, lane_fold(s, jnp.maximum))
        m = jnp.max(m_run, axis=-1, keepdims=True)

        def pass2(c, carry):
            l_run, acc = carry
            p = jnp.exp(s_ref[c] - m)
            v_c = v_ref[pl.ds(pl.multiple_of(c * TK, TK), TK), pair * 128:(pair + 1) * 128]
            return l_run + lane_fold(p, jnp.add), acc + _dot(p.astype(BF16), v_c)

        l_run, acc = lax.fori_loop(0, j + 1, pass2,
                                   (jnp.zeros((TQ, 128), F32), jnp.zeros((TQ, 128), F32)))
        return acc / jnp.sum(l_run, axis=-1, keepdims=True)

    for pair in range(N_HEADS // 2):
        both = jnp.where(lane < V_HEAD_DIM, head(2 * pair), head(2 * pair + 1))
        attn_ref[:, pair * 128:(pair + 1) * 128] = both.astype(BF16)

    br_mla = _dot(attn_ref[...], wmla_ref[...])
    merged = p_ref[...] + g_ref[...] * br_mla
    gate = mod_ref[...][5:6]
    o_ref[...] = x_ref[...] + gate * _dot(merged.astype(BF16), wout_ref[...])


def _attn(q, k, v, p, g, x2d, mod, w_mla, w_out, *, batch, seq):
    tokens = x2d.shape[0]
    nq = seq // TQ
    tok_spec = lambda width: pl.BlockSpec((TQ, width), lambda b, j: (b * nq + j, 0))
    seq_spec = lambda width: pl.BlockSpec((seq, width), lambda b, j: (b, 0))
    return pl.pallas_call(
        _attn_kernel,
        grid=(batch, nq),
        in_specs=[
            tok_spec(N_HEADS * HEAD_PAD), seq_spec(N_HEADS * HEAD_PAD), seq_spec(MLA_WIDTH),
            tok_spec(D_MODEL), tok_spec(D_MODEL), tok_spec(D_MODEL),
            pl.BlockSpec((pl.Squeezed(), N_MOD_ROWS, D_MODEL), lambda b, j: (b, 0, 0)),
            _const_spec((MLA_WIDTH, D_MODEL)), _const_spec((D_MODEL, D_MODEL)),
        ],
        out_specs=tok_spec(D_MODEL),
        out_shape=jax.ShapeDtypeStruct((tokens, D_MODEL), F32),
        scratch_shapes=[pltpu.VMEM((seq // TK, TQ, TK), F32), pltpu.VMEM((TQ, MLA_WIDTH), BF16)],
        compiler_params=pltpu.CompilerParams(
            dimension_semantics=("arbitrary", "arbitrary"), vmem_limit_bytes=VMEM_LIMIT),
        name="attn",
    )(q, k, v, p, g, x2d, mod, w_mla, w_out)


def _prep_mixer_weights(w_in, pool_grp, pool_scale, w_pool_proj, q_a_norm, w_q_up, kv_a_norm, w_kv_up,
                        q_norm_nope, q_norm_rope, k_norm_nope, k_norm_rope):
    splits = np.cumsum([POOL_WIDTH, Q_LORA_RANK, KV_LORA_RANK, QK_ROPE_DIM, D_MODEL])
    u_w, q_w, kv_w, kr_w, gp_w, gm_w = jnp.split(w_in, splits, axis=1)
    kr_w = jnp.pad(kr_w, ((0, 0), (QK_NOPE_DIM, HEAD_PAD - QK_NOPE_DIM - QK_ROPE_DIM)))
    w_in_p = jnp.concatenate([u_w, q_w, kr_w, kv_w, gp_w, gm_w], axis=1).astype(BF16)

    qk_head = QK_NOPE_DIM + QK_ROPE_DIM
    w_q = jnp.pad(w_q_up.reshape(Q_LORA_RANK, N_HEADS, qk_head), ((0, 0), (0, 0), (0, HEAD_PAD - qk_head)))
    w_q = w_q.reshape(Q_LORA_RANK, N_HEADS * HEAD_PAD).astype(BF16)
    kv = w_kv_up.reshape(KV_LORA_RANK, N_HEADS, QK_NOPE_DIM + V_HEAD_DIM)
    w_k = jnp.pad(kv[..., :QK_NOPE_DIM], ((0, 0), (0, 0), (0, HEAD_PAD - QK_NOPE_DIM)))
    w_kv = jnp.concatenate([w_k.reshape(KV_LORA_RANK, N_HEADS * HEAD_PAD),
                            kv[..., QK_NOPE_DIM:].reshape(KV_LORA_RANK, MLA_WIDTH)], axis=1).astype(BF16)

    zeros = lambda n: jnp.zeros((n,), F32)
    q_gain = jnp.tile(jnp.concatenate([q_norm_nope, q_norm_rope, zeros(HEAD_PAD - qk_head)]), N_HEADS)
    k_gain = jnp.tile(jnp.concatenate([k_norm_nope, zeros(HEAD_PAD - QK_NOPE_DIM)]), N_HEADS)
    kr_gain = jnp.concatenate([zeros(QK_NOPE_DIM), k_norm_rope, zeros(HEAD_PAD - qk_head)])

    seg = np.zeros((2 * HEAD_PAD, 2 * HEAD_PAD), np.float32)
    for base in (0, HEAD_PAD):
        seg[base:base + QK_NOPE_DIM, base:base + QK_NOPE_DIM] = 1.0 / QK_NOPE_DIM
        seg[base + QK_NOPE_DIM:base + qk_head, base + QK_NOPE_DIM:base + qk_head] = 1.0 / QK_ROPE_DIM

    inv_freq = ROPE_THETA ** (-jnp.arange(0, QK_ROPE_DIM, 2, dtype=F32) / QK_ROPE_DIM)
    freq = jnp.concatenate([zeros(QK_NOPE_DIM), inv_freq, inv_freq, zeros(HEAD_PAD - qk_head)])

    return {
        "w_in": w_in_p, "pool_grp": pool_grp.astype(BF16), "pool_scale": pool_scale.reshape(1, POOL_WIDTH),
        "w_pool_proj": w_pool_proj.astype(BF16), "q_a_norm": q_a_norm.reshape(1, Q_LORA_RANK), "w_q": w_q,
        "kv_a_norm": kv_a_norm.reshape(1, KV_LORA_RANK), "w_kv": w_kv,
        "q_gain": q_gain.reshape(1, -1), "k_gain": k_gain.reshape(1, -1), "kr_gain": kr_gain.reshape(1, -1),
        "seg": jnp.asarray(seg, BF16), "freq": freq.reshape(1, HEAD_PAD),
    }


def kernel(x, c, positions, w_ada, b_ada, norm_ffn1, w_ffn1_in, w_ffn1_out, norm_mix, w_in, pool_grp,
           pool_scale, w_pool_proj, q_a_norm, w_q_up, kv_a_norm, w_kv_up, q_norm_nope, q_norm_rope,
           k_norm_nope, k_norm_rope, w_mla_proj, w_out, norm_ffn2, w_ffn2_in, w_ffn2_out):
    batch, seq, d = x.shape
    depth = w_ada.shape[0]
    assert d == D_MODEL and seq % TM_FFN == 0 and seq % TM_MIX == 0 and seq % TQ == 0 and TQ == TK
    x2d = x.reshape(batch * seq, d)
    pos2d = positions.reshape(batch * seq, 1)
    for l in range(depth):
        mod = _ada(c, w_ada[l], b_ada[l]).reshape(batch, N_MOD_ROWS, d)
        x2d = _ffn(x2d, mod, norm_ffn1[l].reshape(1, d), w_ffn1_in[l].astype(BF16),
                   w_ffn1_out[l].astype(BF16), sub=0, seq=seq)
        w = _prep_mixer_weights(w_in[l], pool_grp[l], pool_scale[l], w_pool_proj[l], q_a_norm[l], w_q_up[l],
                                kv_a_norm[l], w_kv_up[l], q_norm_nope[l], q_norm_rope[l], k_norm_nope[l],
                                k_norm_rope[l])
        q, k, v, p, g = _mix_in(x2d, mod, pos2d, norm_mix[l].reshape(1, d), w, seq=seq)
        x2d = _attn(q, k, v, p, g, x2d, mod, w_mla_proj[l].astype(BF16), w_out[l].astype(BF16),
                    batch=batch, seq=seq)
        x2d = _ffn(x2d, mod, norm_ffn2[l].reshape(1, d), w_ffn2_in[l].astype(BF16),
                   w_ffn2_out[l].astype(BF16), sub=2, seq=seq)
    return x2d.reshape(batch, seq, d)
```

```python
import functools
import math

import numpy as np
import jax
import jax.numpy as jnp
from jax import lax
from jax.experimental import pallas as pl
from jax.experimental.pallas import tpu as pltpu

F32 = jnp.float32
BF16 = jnp.bfloat16

D_MODEL = 1024
D_FF = 2816
N_MOD_ROWS = 9
POOL_WINDOWS = (2, 4, 8, 16)
POOL_WIDTH = 512
POOL_GROUP_DIM = 128
POOL_HALO = 16
N_HEADS = 8
QK_NOPE_DIM = 64
QK_ROPE_DIM = 32
V_HEAD_DIM = 64
HEAD_PAD = 128
Q_LORA_RANK = 384
KV_LORA_RANK = 256
MLA_WIDTH = N_HEADS * V_HEAD_DIM
ROPE_THETA = 10000.0
ATTN_SCALE = 1.0 / math.sqrt(QK_NOPE_DIM + QK_ROPE_DIM)
NORM_EPS = 1e-6
MASK_VALUE = -1e30

OFF_U = 0
OFF_QK = POOL_WIDTH
OFF_KV = OFF_QK + Q_LORA_RANK + HEAD_PAD
OFF_GP = OFF_KV + KV_LORA_RANK
OFF_GM = OFF_GP + D_MODEL
IN_WIDTH_PAD = OFF_GM + D_MODEL

TM_FFN = 512
TM_MIX = 512
TQ = 256
TK = 256
KLEN_STEP = 512
FF_CHUNK = 256
ADA_BLOCK = 1152
VMEM_LIMIT = 56 * 1024 * 1024


def _dot(a, b):
    return jnp.dot(a, b, preferred_element_type=F32)


def _rms_norm(x, g):
    return x * lax.rsqrt(jnp.mean(x * x, axis=-1, keepdims=True) + NORM_EPS) * g


def _norm_mod(x, g, shift, scale):
    return _rms_norm(x, g) * (1.0 + scale) + shift


def _ada_kernel(c_ref, w_ref, b_ref, o_ref):
    c = c_ref[...]
    c_act = (c * jax.nn.sigmoid(c)).astype(BF16)
    o_ref[...] = _dot(c_act, w_ref[...].astype(BF16)) + b_ref[...]


def _ada(c, w_ada, b_ada):
    batch = c.shape[0]
    n = w_ada.shape[1]
    return pl.pallas_call(
        _ada_kernel,
        grid=(n // ADA_BLOCK,),
        in_specs=[
            pl.BlockSpec((batch, D_MODEL), lambda i: (0, 0)),
            pl.BlockSpec((D_MODEL, ADA_BLOCK), lambda i: (0, i)),
            pl.BlockSpec((1, ADA_BLOCK), lambda i: (0, i)),
        ],
        out_specs=pl.BlockSpec((batch, ADA_BLOCK), lambda i: (0, i)),
        out_shape=jax.ShapeDtypeStruct((batch, n), F32),
        compiler_params=pltpu.CompilerParams(
            dimension_semantics=("arbitrary",), vmem_limit_bytes=VMEM_LIMIT),
        name="ada",
    )(c, w_ada, b_ada.reshape(1, n))


def _ffn_kernel(x_ref, mod_ref, g_ref, win_ref, wout_ref, o_ref, act_ref, *, sub):
    x = x_ref[...]
    mod = mod_ref[...]
    shift, scale, gate = mod[3 * sub:3 * sub + 1], mod[3 * sub + 1:3 * sub + 2], mod[3 * sub + 2:3 * sub + 3]
    h = _norm_mod(x, g_ref[...], shift, scale).astype(BF16)
    for c in range(D_FF // FF_CHUNK):
        lo = c * FF_CHUNK
        g = _dot(h, win_ref[:, lo:lo + FF_CHUNK])
        u = _dot(h, win_ref[:, D_FF + lo:D_FF + lo + FF_CHUNK])
        act_ref[:, lo:lo + FF_CHUNK] = (g * jax.nn.sigmoid(g) * u).astype(BF16)
    y = _dot(act_ref[...], wout_ref[...])
    o_ref[...] = x + (0.5 * gate) * y


def _const_spec(shape):
    return pl.BlockSpec(shape, lambda *_: (0,) * len(shape), pipeline_mode=pl.Buffered(1))


def _ffn(x2d, mod, g, w_in, w_out, *, sub, seq):
    tokens = x2d.shape[0]
    tiles_per_seq = seq // TM_FFN
    return pl.pallas_call(
        functools.partial(_ffn_kernel, sub=sub),
        grid=(tokens // TM_FFN,),
        in_specs=[
            pl.BlockSpec((TM_FFN, D_MODEL), lambda i: (i, 0)),
            pl.BlockSpec((pl.Squeezed(), N_MOD_ROWS, D_MODEL), lambda i: (i // tiles_per_seq, 0, 0)),
            _const_spec((1, D_MODEL)),
            _const_spec((D_MODEL, 2 * D_FF)),
            _const_spec((D_FF, D_MODEL)),
        ],
        out_specs=pl.BlockSpec((TM_FFN, D_MODEL), lambda i: (i, 0)),
        out_shape=jax.ShapeDtypeStruct((tokens, D_MODEL), F32),
        scratch_shapes=[pltpu.VMEM((TM_FFN, D_FF), BF16)],
        compiler_params=pltpu.CompilerParams(
            dimension_semantics=("arbitrary",), vmem_limit_bytes=VMEM_LIMIT),
        name=f"ffn{sub}",
    )(x2d, mod, g, w_in, w_out)


def _segment_mean_sq(z, seg):
    sq = z * z
    hi = sq.astype(BF16)
    lo = (sq - hi.astype(F32)).astype(BF16)
    width = seg.shape[0]
    outs = []
    for p in range(z.shape[1] // width):
        sl = slice(p * width, (p + 1) * width)
        outs.append(_dot(hi[:, sl], seg) + _dot(lo[:, sl], seg))
    return jnp.concatenate(outs, axis=1)


def _rope_head(z, cos_t, sin_lo, sin_hi):
    up = pltpu.roll(z, HEAD_PAD - QK_ROPE_DIM // 2, 1)
    down = pltpu.roll(z, QK_ROPE_DIM // 2, 1)
    return z * cos_t + up * sin_lo + down * sin_hi


def _mix_in_kernel(x_ref, mod_ref, pos_ref, gmix_ref, win_ref, pgrp_ref, pscale_ref, wpp_ref,
                   qan_ref, wq_ref, kvan_ref, wkv_ref, qg_ref, kg_ref, krg_ref, seg_ref, freq_ref,
                   q_out, kt_out, v_out, p_out, g_out, ext_ref, *, tiles_per_seq):
    tm = x_ref.shape[0]
    tile_in_seq = pl.program_id(0) % tiles_per_seq
    x = x_ref[...]
    mod = mod_ref[...]
    h = _norm_mod(x, gmix_ref[...], mod[3:4], mod[4:5]).astype(BF16)

    u = _dot(h, win_ref[:, OFF_U:OFF_U + POOL_WIDTH])

    @pl.when(tile_in_seq == 0)
    def _():
        ext_ref[0:POOL_HALO, :] = jnp.zeros((POOL_HALO, POOL_WIDTH), F32)

    @pl.when(tile_in_seq != 0)
    def _():
        ext_ref[0:POOL_HALO, :] = ext_ref[tm:tm + POOL_HALO, :]

    ext_ref[POOL_HALO:POOL_HALO + tm, :] = u
    t_in_seq = tile_in_seq * tm + lax.broadcasted_iota(jnp.int32, (tm, 1), 0)
    pooled = []
    for grp, window in enumerate(POOL_WINDOWS):
        lanes = slice(grp * POOL_GROUP_DIM, (grp + 1) * POOL_GROUP_DIM)
        u_g = u[:, lanes]
        acc = u_g
        for back in range(1, window):
            acc = acc + ext_ref[POOL_HALO - back:POOL_HALO - back + tm, lanes]
        cnt = jnp.minimum(t_in_seq + 1, window).astype(F32)
        pooled.append(_dot((acc / cnt - u_g).astype(BF16), pgrp_ref[grp]))
    pooled = jnp.concatenate(pooled, axis=1) * pscale_ref[...]
    br_pool = _dot(pooled.astype(BF16), wpp_ref[...])
    g_pool = _dot(h, win_ref[:, OFF_GP:OFF_GP + D_MODEL])
    p_out[...] = jax.nn.sigmoid(g_pool) * br_pool
    g_mla = _dot(h, win_ref[:, OFF_GM:OFF_GM + D_MODEL])
    g_out[...] = jax.nn.sigmoid(g_mla)

    seg = seg_ref[...]
    lane = lax.broadcasted_iota(jnp.int32, (1, HEAD_PAD), 1)
    ang = pos_ref[...].astype(F32) * freq_ref[...]
    cos_a, sin_a = jnp.cos(ang), jnp.sin(ang)
    half = QK_ROPE_DIM // 2
    cos_t = jnp.where(lane < QK_NOPE_DIM, 1.0, cos_a)
    sin_lo = jnp.where((lane >= QK_NOPE_DIM) & (lane < QK_NOPE_DIM + half), -sin_a, 0.0)
    sin_hi = jnp.where((lane >= QK_NOPE_DIM + half) & (lane < QK_NOPE_DIM + 2 * half), sin_a, 0.0)

    qk = _dot(h, win_ref[:, OFF_QK:OFF_QK + Q_LORA_RANK + HEAD_PAD])
    q_lat, k_rope = qk[:, :Q_LORA_RANK], qk[:, Q_LORA_RANK:]
    q = _dot(_rms_norm(q_lat, qan_ref[...]).astype(BF16), wq_ref[...])
    qn = q * lax.rsqrt(_segment_mean_sq(q, seg) + NORM_EPS) * qg_ref[...]
    for hd in range(N_HEADS):
        lanes = slice(hd * HEAD_PAD, (hd + 1) * HEAD_PAD)
        q_out[:, lanes] = (_rope_head(qn[:, lanes], cos_t, sin_lo, sin_hi) * ATTN_SCALE).astype(BF16)

    kv_lat = _dot(h, win_ref[:, OFF_KV:OFF_KV + KV_LORA_RANK])
    kvn = _rms_norm(kv_lat, kvan_ref[...]).astype(BF16)
    k_nope = _dot(kvn, wkv_ref[:, 0:N_HEADS * HEAD_PAD])
    v_out[...] = _dot(kvn, wkv_ref[:, N_HEADS * HEAD_PAD:]).astype(BF16)
    kn = k_nope * lax.rsqrt(_segment_mean_sq(k_nope, seg) + NORM_EPS) * kg_ref[...]
    kr_ms = jnp.sum(k_rope * k_rope, axis=-1, keepdims=True) * (1.0 / QK_ROPE_DIM)
    kr = _rope_head(k_rope * lax.rsqrt(kr_ms + NORM_EPS) * krg_ref[...], cos_t, sin_lo, sin_hi)
    for hd in range(N_HEADS):
        lanes = slice(hd * HEAD_PAD, (hd + 1) * HEAD_PAD)
        kt_out[lanes, :] = (kn[:, lanes] + kr).T.astype(BF16)


def _mix_in(x2d, mod, pos2d, gmix, w, *, seq):
    tokens = x2d.shape[0]
    tm = TM_MIX
    tiles_per_seq = seq // tm
    tok_spec = lambda width: pl.BlockSpec((tm, width), lambda i: (i, 0))
    consts = [gmix, w["w_in"], w["pool_grp"], w["pool_scale"], w["w_pool_proj"], w["q_a_norm"], w["w_q"],
              w["kv_a_norm"], w["w_kv"], w["q_gain"], w["k_gain"], w["kr_gain"], w["seg"], w["freq"]]
    return pl.pallas_call(
        functools.partial(_mix_in_kernel, tiles_per_seq=tiles_per_seq),
        grid=(tokens // tm,),
        in_specs=[
            tok_spec(D_MODEL),
            pl.BlockSpec((pl.Squeezed(), N_MOD_ROWS, D_MODEL), lambda i: (i // tiles_per_seq, 0, 0)),
            tok_spec(1),
        ] + [_const_spec(a.shape) for a in consts],
        out_specs=[tok_spec(N_HEADS * HEAD_PAD),
                   pl.BlockSpec((pl.Squeezed(), N_HEADS * HEAD_PAD, tm),
                                lambda i: (i // tiles_per_seq, 0, i % tiles_per_seq)),
                   tok_spec(MLA_WIDTH), tok_spec(D_MODEL), tok_spec(D_MODEL)],
        out_shape=[
            jax.ShapeDtypeStruct((tokens, N_HEADS * HEAD_PAD), BF16),
            jax.ShapeDtypeStruct((tokens // seq, N_HEADS * HEAD_PAD, seq), BF16),
            jax.ShapeDtypeStruct((tokens, MLA_WIDTH), BF16),
            jax.ShapeDtypeStruct((tokens, D_MODEL), F32),
            jax.ShapeDtypeStruct((tokens, D_MODEL), F32),
        ],
        scratch_shapes=[pltpu.VMEM((POOL_HALO + tm, POOL_WIDTH), F32)],
        compiler_params=pltpu.CompilerParams(
            dimension_semantics=("arbitrary",), vmem_limit_bytes=VMEM_LIMIT),
        name="mix_in",
    )(x2d, mod, pos2d, *consts)


def _attn_kernel(q_ref, kt_ref, v_ref, p_ref, g_ref, x_ref, mod_ref, wmla_ref, wout_ref, o_ref,
                 s_ref, e_ref, attn_ref):
    j = pl.program_id(1)
    seq = kt_ref.shape[1]
    row = lax.broadcasted_iota(jnp.int32, (TQ, TK), 0)
    col = lax.broadcasted_iota(jnp.int32, (TQ, TK), 1)
    lane = lax.broadcasted_iota(jnp.int32, (1, 2 * V_HEAD_DIM), 1)

    def lane_fold(s, op):
        out = s[:, 0:128]
        for t in range(1, TK // 128):
            out = op(out, s[:, t * 128:(t + 1) * 128])
        return out

    def head(hd, klen):
        slot = hd % 2
        rows = slice(hd * HEAD_PAD, (hd + 1) * HEAD_PAD)
        q_h = q_ref[:, rows]
        n_chunks = klen // TK
        m_run = None
        for c in range(n_chunks):
            cols = slice(c * TK, (c + 1) * TK)
            s = _dot(q_h, kt_ref[rows, cols])
            if (c + 1) * TK > klen - KLEN_STEP:
                s = jnp.where(col + (c * TK - j * TQ) <= row, s, MASK_VALUE)
            s_ref[slot, :, cols] = s
            fold = lane_fold(s, jnp.maximum)
            m_run = fold if m_run is None else jnp.maximum(m_run, fold)
        m = jnp.max(m_run, axis=-1, keepdims=True)
        l_run = None
        for c in range(n_chunks):
            cols = slice(c * TK, (c + 1) * TK)
            e = jnp.exp(s_ref[slot, :, cols] - m)
            e_ref[slot, :, cols] = e.astype(BF16)
            fold = lane_fold(e, jnp.add)
            l_run = fold if l_run is None else l_run + fold
        pair = hd // 2
        acc = _dot(e_ref[slot, :, 0:klen], v_ref[0:klen, pair * 128:(pair + 1) * 128])
        return acc / jnp.sum(l_run, axis=-1, keepdims=True)

    for variant in range(seq // KLEN_STEP):
        @pl.when((j * TQ) // KLEN_STEP == variant)
        def _(variant=variant):
            klen = (variant + 1) * KLEN_STEP
            for pair in range(N_HEADS // 2):
                both = jnp.where(lane < V_HEAD_DIM, head(2 * pair, klen), head(2 * pair + 1, klen))
                attn_ref[:, pair * 128:(pair + 1) * 128] = both.astype(BF16)

    br_mla = _dot(attn_ref[...], wmla_ref[...])
    merged = p_ref[...] + g_ref[...] * br_mla
    gate = mod_ref[...][5:6]
    o_ref[...] = x_ref[...] + gate * _dot(merged.astype(BF16), wout_ref[...])


def _attn(q, kt, v, p, g, x2d, mod, w_mla, w_out, *, batch, seq):
    tokens = x2d.shape[0]
    nq = seq // TQ
    tok_spec = lambda width: pl.BlockSpec((TQ, width), lambda b, j: (b * nq + j, 0))
    return pl.pallas_call(
        _attn_kernel,
        grid=(batch, nq),
        in_specs=[
            tok_spec(N_HEADS * HEAD_PAD),
            pl.BlockSpec((pl.Squeezed(), N_HEADS * HEAD_PAD, seq), lambda b, j: (b, 0, 0)),
            pl.BlockSpec((seq, MLA_WIDTH), lambda b, j: (b, 0)),
            tok_spec(D_MODEL), tok_spec(D_MODEL), tok_spec(D_MODEL),
            pl.BlockSpec((pl.Squeezed(), N_MOD_ROWS, D_MODEL), lambda b, j: (b, 0, 0)),
            _const_spec((MLA_WIDTH, D_MODEL)), _const_spec((D_MODEL, D_MODEL)),
        ],
        out_specs=tok_spec(D_MODEL),
        out_shape=jax.ShapeDtypeStruct((tokens, D_MODEL), F32),
        scratch_shapes=[pltpu.VMEM((2, TQ, seq), F32), pltpu.VMEM((2, TQ, seq), BF16),
                        pltpu.VMEM((TQ, MLA_WIDTH), BF16)],
        compiler_params=pltpu.CompilerParams(
            dimension_semantics=("arbitrary", "arbitrary"), vmem_limit_bytes=VMEM_LIMIT),
        name="attn",
    )(q, kt, v, p, g, x2d, mod, w_mla, w_out)


def _prep_mixer_weights(w_in, pool_grp, pool_scale, w_pool_proj, q_a_norm, w_q_up, kv_a_norm, w_kv_up,
                        q_norm_nope, q_norm_rope, k_norm_nope, k_norm_rope):
    splits = np.cumsum([POOL_WIDTH, Q_LORA_RANK, KV_LORA_RANK, QK_ROPE_DIM, D_MODEL])
    u_w, q_w, kv_w, kr_w, gp_w, gm_w = jnp.split(w_in, splits, axis=1)
    kr_w = jnp.pad(kr_w, ((0, 0), (QK_NOPE_DIM, HEAD_PAD - QK_NOPE_DIM - QK_ROPE_DIM)))
    w_in_p = jnp.concatenate([u_w, q_w, kr_w, kv_w, gp_w, gm_w], axis=1).astype(BF16)

    qk_head = QK_NOPE_DIM + QK_ROPE_DIM
    w_q = jnp.pad(w_q_up.reshape(Q_LORA_RANK, N_HEADS, qk_head), ((0, 0), (0, 0), (0, HEAD_PAD - qk_head)))
    w_q = w_q.reshape(Q_LORA_RANK, N_HEADS * HEAD_PAD).astype(BF16)
    kv = w_kv_up.reshape(KV_LORA_RANK, N_HEADS, QK_NOPE_DIM + V_HEAD_DIM)
    w_k = jnp.pad(kv[..., :QK_NOPE_DIM], ((0, 0), (0, 0), (0, HEAD_PAD - QK_NOPE_DIM)))
    w_kv = jnp.concatenate([w_k.reshape(KV_LORA_RANK, N_HEADS * HEAD_PAD),
                            kv[..., QK_NOPE_DIM:].reshape(KV_LORA_RANK, MLA_WIDTH)], axis=1).astype(BF16)

    zeros = lambda n: jnp.zeros((n,), F32)
    q_gain = jnp.tile(jnp.concatenate([q_norm_nope, q_norm_rope, zeros(HEAD_PAD - qk_head)]), N_HEADS)
    k_gain = jnp.tile(jnp.concatenate([k_norm_nope, zeros(HEAD_PAD - QK_NOPE_DIM)]), N_HEADS)
    kr_gain = jnp.concatenate([zeros(QK_NOPE_DIM), k_norm_rope, zeros(HEAD_PAD - qk_head)])

    seg = np.zeros((2 * HEAD_PAD, 2 * HEAD_PAD), np.float32)
    for base in (0, HEAD_PAD):
        seg[base:base + QK_NOPE_DIM, base:base + QK_NOPE_DIM] = 1.0 / QK_NOPE_DIM
        seg[base + QK_NOPE_DIM:base + qk_head, base + QK_NOPE_DIM:base + qk_head] = 1.0 / QK_ROPE_DIM

    inv_freq = ROPE_THETA ** (-jnp.arange(0, QK_ROPE_DIM, 2, dtype=F32) / QK_ROPE_DIM)
    freq = jnp.concatenate([zeros(QK_NOPE_DIM), inv_freq, inv_freq, zeros(HEAD_PAD - qk_head)])

    return {
        "w_in": w_in_p, "pool_grp": pool_grp.astype(BF16), "pool_scale": pool_scale.reshape(1, POOL_WIDTH),
        "w_pool_proj": w_pool_proj.astype(BF16), "q_a_norm": q_a_norm.reshape(1, Q_LORA_RANK), "w_q": w_q,
        "kv_a_norm": kv_a_norm.reshape(1, KV_LORA_RANK), "w_kv": w_kv,
        "q_gain": q_gain.reshape(1, -1), "k_gain": k_gain.reshape(1, -1), "kr_gain": kr_gain.reshape(1, -1),
        "seg": jnp.asarray(seg, BF16), "freq": freq.reshape(1, HEAD_PAD),
    }


def kernel(x, c, positions, w_ada, b_ada, norm_ffn1, w_ffn1_in, w_ffn1_out, norm_mix, w_in, pool_grp,
           pool_scale, w_pool_proj, q_a_norm, w_q_up, kv_a_norm, w_kv_up, q_norm_nope, q_norm_rope,
           k_norm_nope, k_norm_rope, w_mla_proj, w_out, norm_ffn2, w_ffn2_in, w_ffn2_out):
    batch, seq, d = x.shape
    depth = w_ada.shape[0]
    assert d == D_MODEL and seq % TM_FFN == 0 and seq % TM_MIX == 0
    assert seq % KLEN_STEP == 0 and KLEN_STEP % TQ == 0 and KLEN_STEP % TK == 0
    x2d = x.reshape(batch * seq, d)
    pos2d = positions.reshape(batch * seq, 1)
    for l in range(depth):
        mod = _ada(c, w_ada[l], b_ada[l]).reshape(batch, N_MOD_ROWS, d)
        x2d = _ffn(x2d, mod, norm_ffn1[l].reshape(1, d), w_ffn1_in[l].astype(BF16),
                   w_ffn1_out[l].astype(BF16), sub=0, seq=seq)
        w = _prep_mixer_weights(w_in[l], pool_grp[l], pool_scale[l], w_pool_proj[l], q_a_norm[l], w_q_up[l],
                                kv_a_norm[l], w_kv_up[l], q_norm_nope[l], q_norm_rope[l], k_norm_nope[l],
                                k_norm_rope[l])
        q, kt, v, p, g = _mix_in(x2d, mod, pos2d, norm_mix[l].reshape(1, d), w, seq=seq)
        x2d = _attn(q, kt, v, p, g, x2d, mod, w_mla_proj[l].astype(BF16), w_out[l].astype(BF16),
                    batch=batch, seq=seq)
        x2d = _ffn(x2d, mod, norm_ffn2[l].reshape(1, d), w_ffn2_in[l].astype(BF16),
                   w_ffn2_out[l].astype(BF16), sub=2, seq=seq)
    return x2d.reshape(batch, seq, d)
```

```python
import functools
import math

import numpy as np
import jax
import jax.numpy as jnp
from jax import lax
from jax.experimental import pallas as pl
from jax.experimental.pallas import tpu as pltpu

F32 = jnp.float32
BF16 = jnp.bfloat16

D_MODEL = 1024
D_FF = 2816
N_MOD_ROWS = 9
POOL_WINDOWS = (2, 4, 8, 16)
POOL_WIDTH = 512
POOL_GROUP_DIM = 128
POOL_HALO = 16
N_HEADS = 8
QK_NOPE_DIM = 64
QK_ROPE_DIM = 32
V_HEAD_DIM = 64
HEAD_PAD = 128
Q_LORA_RANK = 384
KV_LORA_RANK = 256
MLA_WIDTH = N_HEADS * V_HEAD_DIM
ROPE_THETA = 10000.0
ATTN_SCALE = 1.0 / math.sqrt(QK_NOPE_DIM + QK_ROPE_DIM)
Q_SCALE = ATTN_SCALE * math.log2(math.e)
NORM_EPS = 1e-6
MASK_VALUE = -1e30

OFF_U = 0
OFF_QK = POOL_WIDTH
OFF_KV = OFF_QK + Q_LORA_RANK + HEAD_PAD
OFF_GP = OFF_KV + KV_LORA_RANK
OFF_GM = OFF_GP + D_MODEL
IN_WIDTH_PAD = OFF_GM + D_MODEL

TM_FFN = 512
TM_MIX = 512
TQ = 256
TK = 256
KLEN_STEP = 256
FF_CHUNK = 256
ADA_BLOCK = 1152
VMEM_LIMIT = 56 * 1024 * 1024


def _dot(a, b):
    return jnp.dot(a, b, preferred_element_type=F32)


def _rms_norm(x, g):
    return x * lax.rsqrt(jnp.mean(x * x, axis=-1, keepdims=True) + NORM_EPS) * g


def _norm_mod(x, g, shift, scale):
    return _rms_norm(x, g) * (1.0 + scale) + shift


def _ada_kernel(c_ref, w_ref, b_ref, o_ref):
    c = c_ref[...]
    c_act = (c * jax.nn.sigmoid(c)).astype(BF16)
    o_ref[...] = _dot(c_act, w_ref[...].astype(BF16)) + b_ref[...]


def _ada(c, w_ada, b_ada):
    batch = c.shape[0]
    n = w_ada.shape[1]
    return pl.pallas_call(
        _ada_kernel,
        grid=(n // ADA_BLOCK,),
        in_specs=[
            pl.BlockSpec((batch, D_MODEL), lambda i: (0, 0)),
            pl.BlockSpec((D_MODEL, ADA_BLOCK), lambda i: (0, i)),
            pl.BlockSpec((1, ADA_BLOCK), lambda i: (0, i)),
        ],
        out_specs=pl.BlockSpec((batch, ADA_BLOCK), lambda i: (0, i)),
        out_shape=jax.ShapeDtypeStruct((batch, n), F32),
        compiler_params=pltpu.CompilerParams(
            dimension_semantics=("arbitrary",), vmem_limit_bytes=VMEM_LIMIT),
        name="ada",
    )(c, w_ada, b_ada.reshape(1, n))


def _ffn_kernel(x_ref, mod_ref, g_ref, win_ref, wout_ref, o_ref, act_ref, *, sub):
    x = x_ref[...]
    mod = mod_ref[...]
    shift, scale, gate = mod[3 * sub:3 * sub + 1], mod[3 * sub + 1:3 * sub + 2], mod[3 * sub + 2:3 * sub + 3]
    h = _norm_mod(x, g_ref[...], shift, scale).astype(BF16)
    for c in range(D_FF // FF_CHUNK):
        lo = c * FF_CHUNK
        g = _dot(h, win_ref[:, lo:lo + FF_CHUNK])
        u = _dot(h, win_ref[:, D_FF + lo:D_FF + lo + FF_CHUNK])
        act_ref[:, lo:lo + FF_CHUNK] = (g * jax.nn.sigmoid(g) * u).astype(BF16)
    y = _dot(act_ref[...], wout_ref[...])
    o_ref[...] = x + (0.5 * gate) * y


def _const_spec(shape):
    return pl.BlockSpec(shape, lambda *_: (0,) * len(shape), pipeline_mode=pl.Buffered(1))


def _ffn(x2d, mod, g, w_in, w_out, *, sub, seq):
    tokens = x2d.shape[0]
    tiles_per_seq = seq // TM_FFN
    return pl.pallas_call(
        functools.partial(_ffn_kernel, sub=sub),
        grid=(tokens // TM_FFN,),
        in_specs=[
            pl.BlockSpec((TM_FFN, D_MODEL), lambda i: (i, 0)),
            pl.BlockSpec((pl.Squeezed(), N_MOD_ROWS, D_MODEL), lambda i: (i // tiles_per_seq, 0, 0)),
            _const_spec((1, D_MODEL)),
            _const_spec((D_MODEL, 2 * D_FF)),
            _const_spec((D_FF, D_MODEL)),
        ],
        out_specs=pl.BlockSpec((TM_FFN, D_MODEL), lambda i: (i, 0)),
        out_shape=jax.ShapeDtypeStruct((tokens, D_MODEL), F32),
        scratch_shapes=[pltpu.VMEM((TM_FFN, D_FF), BF16)],
        compiler_params=pltpu.CompilerParams(
            dimension_semantics=("arbitrary",), vmem_limit_bytes=VMEM_LIMIT),
        name=f"ffn{sub}",
    )(x2d, mod, g, w_in, w_out)


def _segment_mean_sq(z, seg):
    sq = z * z
    hi = sq.astype(BF16)
    lo = (sq - hi.astype(F32)).astype(BF16)
    width = seg.shape[0]
    outs = []
    for p in range(z.shape[1] // width):
        sl = slice(p * width, (p + 1) * width)
        outs.append(_dot(hi[:, sl], seg) + _dot(lo[:, sl], seg))
    return jnp.concatenate(outs, axis=1)


def _rope_head(z, cos_t, sin_lo, sin_hi):
    up = pltpu.roll(z, HEAD_PAD - QK_ROPE_DIM // 2, 1)
    down = pltpu.roll(z, QK_ROPE_DIM // 2, 1)
    return z * cos_t + up * sin_lo + down * sin_hi


def _mix_in_kernel(*refs, tiles_per_seq):
    *io_refs, ext_ref, q_a, k_a, kr_a, q_b, k_b, kr_b = refs
    step = pl.program_id(0)

    @pl.when(step == 0)
    def _():
        for ref in (q_b, k_b, kr_b):
            ref[...] = jnp.zeros(ref.shape, F32)

    @pl.when(step % 2 == 0)
    def _():
        _mix_in_step(*io_refs, ext_ref, (q_a, k_a, kr_a), (q_b, k_b, kr_b), tiles_per_seq=tiles_per_seq)

    @pl.when(step % 2 == 1)
    def _():
        _mix_in_step(*io_refs, ext_ref, (q_b, k_b, kr_b), (q_a, k_a, kr_a), tiles_per_seq=tiles_per_seq)


def _mix_in_step(x_ref, mod_ref, pos_ref, gmix_ref, win_ref, pgrp_ref, pscale_ref, wpp_ref,
                 qan_ref, wq_ref, kvan_ref, wkv_ref, qg_ref, kg_ref, krg_ref, seg_ref, freq_ref,
                 q_out, kt_out, v_out, p_out, g_out, ext_ref, raw_write, raw_read, *, tiles_per_seq):
    tm = x_ref.shape[0]
    step = pl.program_id(0)
    last_step = pl.num_programs(0) - 1
    tile_in_seq = jnp.minimum(step, last_step - 1) % tiles_per_seq
    q_w, k_w, kr_w = raw_write
    q_r, k_r, kr_r = raw_read

    seg = seg_ref[...]
    pair_width = 2 * HEAD_PAD

    def rope_tables():
        lane = lax.broadcasted_iota(jnp.int32, (1, HEAD_PAD), 1)
        ang = pos_ref[...].astype(F32) * freq_ref[...]
        cos_a, sin_a = jnp.cos(ang), jnp.sin(ang)
        half = QK_ROPE_DIM // 2
        cos_t = jnp.where(lane < QK_NOPE_DIM, 1.0, cos_a)
        sin_lo = jnp.where((lane >= QK_NOPE_DIM) & (lane < QK_NOPE_DIM + half), -sin_a, 0.0)
        sin_hi = jnp.where((lane >= QK_NOPE_DIM + half) & (lane < QK_NOPE_DIM + 2 * half), sin_a, 0.0)
        return cos_t, sin_lo, sin_hi

    def q_piece(pair, tables):
        lanes = slice(pair * pair_width, (pair + 1) * pair_width)
        q = q_r[:, lanes]
        qn = q * lax.rsqrt(_segment_mean_sq(q, seg) + NORM_EPS) * qg_ref[:, lanes]
        for sub in range(2):
            hd = 2 * pair + sub
            roped = _rope_head(qn[:, sub * HEAD_PAD:(sub + 1) * HEAD_PAD], *tables)
            q_out[:, hd * HEAD_PAD:(hd + 1) * HEAD_PAD] = (roped * Q_SCALE).astype(BF16)

    def k_rope_piece(tables):
        k_rope = kr_r[...]
        kr_ms = jnp.sum(k_rope * k_rope, axis=-1, keepdims=True) * (1.0 / QK_ROPE_DIM)
        return _rope_head(k_rope * lax.rsqrt(kr_ms + NORM_EPS) * krg_ref[...], *tables)

    def k_piece(pair, kr):
        lanes = slice(pair * pair_width, (pair + 1) * pair_width)
        k_nope = k_r[:, lanes]
        kn = k_nope * lax.rsqrt(_segment_mean_sq(k_nope, seg) + NORM_EPS) * kg_ref[:, lanes]
        for sub in range(2):
            hd = 2 * pair + sub
            k_h = kn[:, sub * HEAD_PAD:(sub + 1) * HEAD_PAD] + kr
            kt_out[hd * HEAD_PAD:(hd + 1) * HEAD_PAD, :] = k_h.T.astype(BF16)

    x = x_ref[...]
    mod = mod_ref[...]
    h = _norm_mod(x, gmix_ref[...], mod[3:4], mod[4:5]).astype(BF16)

    u = _dot(h, win_ref[:, OFF_U:OFF_U + POOL_WIDTH])
    g_out[...] = jax.nn.sigmoid(_dot(h, win_ref[:, OFF_GM:OFF_GM + D_MODEL]))
    tables = rope_tables()
    kr = k_rope_piece(tables)

    @pl.when(tile_in_seq == 0)
    def _():
        ext_ref[0:POOL_HALO, :] = jnp.zeros((POOL_HALO, POOL_WIDTH), F32)

    @pl.when((tile_in_seq != 0) & (step != last_step))
    def _():
        ext_ref[0:POOL_HALO, :] = ext_ref[tm:tm + POOL_HALO, :]

    ext_ref[POOL_HALO:POOL_HALO + tm, :] = u
    p_out[...] = jax.nn.sigmoid(_dot(h, win_ref[:, OFF_GP:OFF_GP + D_MODEL]))
    q_piece(0, tables)
    k_piece(0, kr)
    qk = _dot(h, win_ref[:, OFF_QK:OFF_QK + Q_LORA_RANK + HEAD_PAD])
    kr_w[...] = qk[:, Q_LORA_RANK:]

    t_in_seq = tile_in_seq * tm + lax.broadcasted_iota(jnp.int32, (tm, 1), 0)
    pooled = []
    for grp, window in enumerate(POOL_WINDOWS):
        lanes = slice(grp * POOL_GROUP_DIM, (grp + 1) * POOL_GROUP_DIM)
        u_g = u[:, lanes]
        acc = u_g
        for back in range(1, window):
            acc = acc + ext_ref[POOL_HALO - back:POOL_HALO - back + tm, lanes]
        cnt = jnp.minimum(t_in_seq + 1, window).astype(F32)
        pooled.append(_dot((acc / cnt - u_g).astype(BF16), pgrp_ref[grp]))
    pooled = jnp.concatenate(pooled, axis=1) * pscale_ref[...]
    p_out[...] = p_out[...] * _dot(pooled.astype(BF16), wpp_ref[...])
    q_piece(1, tables)
    k_piece(1, kr)

    q_w[...] = _dot(_rms_norm(qk[:, :Q_LORA_RANK], qan_ref[...]).astype(BF16), wq_ref[...])
    q_piece(2, tables)
    k_piece(2, kr)
    kv_lat = _dot(h, win_ref[:, OFF_KV:OFF_KV + KV_LORA_RANK])
    kvn = _rms_norm(kv_lat, kvan_ref[...]).astype(BF16)
    k_w[...] = _dot(kvn, wkv_ref[:, 0:N_HEADS * HEAD_PAD])
    q_piece(3, tables)
    k_piece(3, kr)
    v_out[...] = _dot(kvn, wkv_ref[:, N_HEADS * HEAD_PAD:]).astype(BF16)


def _mix_in(x2d, mod, pos2d, gmix, w, *, seq):
    tokens = x2d.shape[0]
    tm = TM_MIX
    tiles_per_seq = seq // tm
    n_tiles = tokens // tm
    cur = lambda i: jnp.minimum(i, n_tiles - 1)
    prev = lambda i: jnp.maximum(i - 1, 0)
    cur_spec = lambda width: pl.BlockSpec((tm, width), lambda i: (cur(i), 0))
    prev_spec = lambda width: pl.BlockSpec((tm, width), lambda i: (prev(i), 0))
    consts = [gmix, w["w_in"], w["pool_grp"], w["pool_scale"], w["w_pool_proj"], w["q_a_norm"], w["w_q"],
              w["kv_a_norm"], w["w_kv"], w["q_gain"], w["k_gain"], w["kr_gain"], w["seg"], w["freq"]]
    return pl.pallas_call(
        functools.partial(_mix_in_kernel, tiles_per_seq=tiles_per_seq),
        grid=(n_tiles + 1,),
        in_specs=[
            cur_spec(D_MODEL),
            pl.BlockSpec((pl.Squeezed(), N_MOD_ROWS, D_MODEL), lambda i: (cur(i) // tiles_per_seq, 0, 0)),
            prev_spec(1),
        ] + [_const_spec(a.shape) for a in consts],
        out_specs=[prev_spec(N_HEADS * HEAD_PAD),
                   pl.BlockSpec((pl.Squeezed(), N_HEADS * HEAD_PAD, tm),
                                lambda i: (prev(i) // tiles_per_seq, 0, prev(i) % tiles_per_seq)),
                   cur_spec(MLA_WIDTH), cur_spec(D_MODEL), cur_spec(D_MODEL)],
        out_shape=[
            jax.ShapeDtypeStruct((tokens, N_HEADS * HEAD_PAD), BF16),
            jax.ShapeDtypeStruct((tokens // seq, N_HEADS * HEAD_PAD, seq), BF16),
            jax.ShapeDtypeStruct((tokens, MLA_WIDTH), BF16),
            jax.ShapeDtypeStruct((tokens, D_MODEL), F32),
            jax.ShapeDtypeStruct((tokens, D_MODEL), F32),
        ],
        scratch_shapes=[pltpu.VMEM((POOL_HALO + tm, POOL_WIDTH), F32)] + 2 * [
            pltpu.VMEM((tm, N_HEADS * HEAD_PAD), F32), pltpu.VMEM((tm, N_HEADS * HEAD_PAD), F32),
            pltpu.VMEM((tm, HEAD_PAD), F32)],
        compiler_params=pltpu.CompilerParams(
            dimension_semantics=("arbitrary",), vmem_limit_bytes=VMEM_LIMIT),
        name="mix_in",
    )(x2d, mod, pos2d, *consts)


def _attn_kernel(q_ref, kt_ref, v_ref, p_ref, g_ref, x_ref, mod_ref, wmla_ref, wout_ref, o_ref,
                 s_ref, e_ref, attn_ref):
    j = pl.program_id(1)
    seq = kt_ref.shape[1]
    row = lax.broadcasted_iota(jnp.int32, (TQ, TK), 0)
    col = lax.broadcasted_iota(jnp.int32, (TQ, TK), 1)
    lane = lax.broadcasted_iota(jnp.int32, (1, 2 * V_HEAD_DIM), 1)

    def lane_fold(s, op):
        out = s[:, 0:128]
        for t in range(1, TK // 128):
            out = op(out, s[:, t * 128:(t + 1) * 128])
        return out

    def head(hd, klen):
        slot = hd % 2
        rows = slice(hd * HEAD_PAD, (hd + 1) * HEAD_PAD)
        q_h = q_ref[:, rows]
        n_chunks = klen // TK
        m_run = None
        for c in range(n_chunks):
            cols = slice(c * TK, (c + 1) * TK)
            s = _dot(q_h, kt_ref[rows, cols])
            if (c + 1) * TK > klen - KLEN_STEP:
                s = jnp.where(col + (c * TK - j * TQ) <= row, s, MASK_VALUE)
            s_ref[slot, :, cols] = s
            fold = lane_fold(s, jnp.maximum)
            m_run = fold if m_run is None else jnp.maximum(m_run, fold)
        m = jnp.max(m_run, axis=-1, keepdims=True)
        l_run = None
        for c in range(n_chunks):
            cols = slice(c * TK, (c + 1) * TK)
            e = jnp.exp2(s_ref[slot, :, cols] - m)
            e_ref[slot, :, cols] = e.astype(BF16)
            fold = lane_fold(e, jnp.add)
            l_run = fold if l_run is None else l_run + fold
        pair = hd // 2
        acc = _dot(e_ref[slot, :, 0:klen], v_ref[0:klen, pair * 128:(pair + 1) * 128])
        return acc / jnp.sum(l_run, axis=-1, keepdims=True)

    for variant in range(seq // KLEN_STEP):
        @pl.when((j * TQ) // KLEN_STEP == variant)
        def _(variant=variant):
            klen = (variant + 1) * KLEN_STEP
            for pair in range(N_HEADS // 2):
                both = jnp.where(lane < V_HEAD_DIM, head(2 * pair, klen), head(2 * pair + 1, klen))
                attn_ref[:, pair * 128:(pair + 1) * 128] = both.astype(BF16)

    br_mla = _dot(attn_ref[...], wmla_ref[...])
    merged = p_ref[...] + g_ref[...] * br_mla
    gate = mod_ref[...][5:6]
    o_ref[...] = x_ref[...] + gate * _dot(merged.astype(BF16), wout_ref[...])


def _attn(q, kt, v, p, g, x2d, mod, w_mla, w_out, *, batch, seq):
    tokens = x2d.shape[0]
    nq = seq // TQ
    tok_spec = lambda width: pl.BlockSpec((TQ, width), lambda b, j: (b * nq + j, 0))
    return pl.pallas_call(
        _attn_kernel,
        grid=(batch, nq),
        in_specs=[
            tok_spec(N_HEADS * HEAD_PAD),
            pl.BlockSpec((pl.Squeezed(), N_HEADS * HEAD_PAD, seq), lambda b, j: (b, 0, 0)),
            pl.BlockSpec((seq, MLA_WIDTH), lambda b, j: (b, 0)),
            tok_spec(D_MODEL), tok_spec(D_MODEL), tok_spec(D_MODEL),
            pl.BlockSpec((pl.Squeezed(), N_MOD_ROWS, D_MODEL), lambda b, j: (b, 0, 0)),
            _const_spec((MLA_WIDTH, D_MODEL)), _const_spec((D_MODEL, D_MODEL)),
        ],
        out_specs=tok_spec(D_MODEL),
        out_shape=jax.ShapeDtypeStruct((tokens, D_MODEL), F32),
        scratch_shapes=[pltpu.VMEM((2, TQ, seq), F32), pltpu.VMEM((2, TQ, seq), BF16),
                        pltpu.VMEM((TQ, MLA_WIDTH), BF16)],
        compiler_params=pltpu.CompilerParams(
            dimension_semantics=("arbitrary", "arbitrary"), vmem_limit_bytes=VMEM_LIMIT),
        name="attn",
    )(q, kt, v, p, g, x2d, mod, w_mla, w_out)


def _prep_mixer_weights(w_in, pool_grp, pool_scale, w_pool_proj, q_a_norm, w_q_up, kv_a_norm, w_kv_up,
                        q_norm_nope, q_norm_rope, k_norm_nope, k_norm_rope):
    splits = np.cumsum([POOL_WIDTH, Q_LORA_RANK, KV_LORA_RANK, QK_ROPE_DIM, D_MODEL])
    u_w, q_w, kv_w, kr_w, gp_w, gm_w = jnp.split(w_in, splits, axis=1)
    kr_w = jnp.pad(kr_w, ((0, 0), (QK_NOPE_DIM, HEAD_PAD - QK_NOPE_DIM - QK_ROPE_DIM)))
    w_in_p = jnp.concatenate([u_w, q_w, kr_w, kv_w, gp_w, gm_w], axis=1).astype(BF16)

    qk_head = QK_NOPE_DIM + QK_ROPE_DIM
    w_q = jnp.pad(w_q_up.reshape(Q_LORA_RANK, N_HEADS, qk_head), ((0, 0), (0, 0), (0, HEAD_PAD - qk_head)))
    w_q = w_q.reshape(Q_LORA_RANK, N_HEADS * HEAD_PAD).astype(BF16)
    kv = w_kv_up.reshape(KV_LORA_RANK, N_HEADS, QK_NOPE_DIM + V_HEAD_DIM)
    w_k = jnp.pad(kv[..., :QK_NOPE_DIM], ((0, 0), (0, 0), (0, HEAD_PAD - QK_NOPE_DIM)))
    w_kv = jnp.concatenate([w_k.reshape(KV_LORA_RANK, N_HEADS * HEAD_PAD),
                            kv[..., QK_NOPE_DIM:].reshape(KV_LORA_RANK, MLA_WIDTH)], axis=1).astype(BF16)

    zeros = lambda n: jnp.zeros((n,), F32)
    q_gain = jnp.tile(jnp.concatenate([q_norm_nope, q_norm_rope, zeros(HEAD_PAD - qk_head)]), N_HEADS)
    k_gain = jnp.tile(jnp.concatenate([k_norm_nope, zeros(HEAD_PAD - QK_NOPE_DIM)]), N_HEADS)
    kr_gain = jnp.concatenate([zeros(QK_NOPE_DIM), k_norm_rope, zeros(HEAD_PAD - qk_head)])

    seg = np.zeros((2 * HEAD_PAD, 2 * HEAD_PAD), np.float32)
    for base in (0, HEAD_PAD):
        seg[base:base + QK_NOPE_DIM, base:base + QK_NOPE_DIM] = 1.0 / QK_NOPE_DIM
        seg[base + QK_NOPE_DIM:base + qk_head, base + QK_NOPE_DIM:base + qk_head] = 1.0 / QK_ROPE_DIM

    inv_freq = ROPE_THETA ** (-jnp.arange(0, QK_ROPE_DIM, 2, dtype=F32) / QK_ROPE_DIM)
    freq = jnp.concatenate([zeros(QK_NOPE_DIM), inv_freq, inv_freq, zeros(HEAD_PAD - qk_head)])

    return {
        "w_in": w_in_p, "pool_grp": pool_grp.astype(BF16), "pool_scale": pool_scale.reshape(1, POOL_WIDTH),
        "w_pool_proj": w_pool_proj.astype(BF16), "q_a_norm": q_a_norm.reshape(1, Q_LORA_RANK), "w_q": w_q,
        "kv_a_norm": kv_a_norm.reshape(1, KV_LORA_RANK), "w_kv": w_kv,
        "q_gain": q_gain.reshape(1, -1), "k_gain": k_gain.reshape(1, -1), "kr_gain": kr_gain.reshape(1, -1),
        "seg": jnp.asarray(seg, BF16), "freq": freq.reshape(1, HEAD_PAD),
    }


def kernel(x, c, positions, w_ada, b_ada, norm_ffn1, w_ffn1_in, w_ffn1_out, norm_mix, w_in, pool_grp,
           pool_scale, w_pool_proj, q_a_norm, w_q_up, kv_a_norm, w_kv_up, q_norm_nope, q_norm_rope,
           k_norm_nope, k_norm_rope, w_mla_proj, w_out, norm_ffn2, w_ffn2_in, w_ffn2_out):
    batch, seq, d = x.shape
    depth = w_ada.shape[0]
    assert d == D_MODEL and seq % TM_FFN == 0 and seq % TM_MIX == 0
    assert seq % KLEN_STEP == 0 and KLEN_STEP % TQ == 0 and KLEN_STEP % TK == 0
    x2d = x.reshape(batch * seq, d)
    pos2d = positions.reshape(batch * seq, 1)
    for l in range(depth):
        mod = _ada(c, w_ada[l], b_ada[l]).reshape(batch, N_MOD_ROWS, d)
        x2d = _ffn(x2d, mod, norm_ffn1[l].reshape(1, d), w_ffn1_in[l].astype(BF16),
                   w_ffn1_out[l].astype(BF16), sub=0, seq=seq)
        w = _prep_mixer_weights(w_in[l], pool_grp[l], pool_scale[l], w_pool_proj[l], q_a_norm[l], w_q_up[l],
                                kv_a_norm[l], w_kv_up[l], q_norm_nope[l], q_norm_rope[l], k_norm_nope[l],
                                k_norm_rope[l])
        q, kt, v, p, g = _mix_in(x2d, mod, pos2d, norm_mix[l].reshape(1, d), w, seq=seq)
        x2d = _attn(q, kt, v, p, g, x2d, mod, w_mla_proj[l].astype(BF16), w_out[l].astype(BF16),
                    batch=batch, seq=seq)
        x2d = _ffn(x2d, mod, norm_ffn2[l].reshape(1, d), w_ffn2_in[l].astype(BF16),
                   w_ffn2_out[l].astype(BF16), sub=2, seq=seq)
    return x2d.reshape(batch, seq, d)
```

```python
import functools
import math

import numpy as np
import jax
import jax.numpy as jnp
from jax import lax
from jax.experimental import pallas as pl
from jax.experimental.pallas import tpu as pltpu

F32 = jnp.float32
BF16 = jnp.bfloat16

D_MODEL = 1024
D_FF = 2816
N_MOD_ROWS = 9
POOL_WINDOWS = (2, 4, 8, 16)
POOL_WIDTH = 512
POOL_GROUP_DIM = 128
POOL_HALO = 16
N_HEADS = 8
QK_NOPE_DIM = 64
QK_ROPE_DIM = 32
V_HEAD_DIM = 64
HEAD_PAD = 128
Q_LORA_RANK = 384
KV_LORA_RANK = 256
MLA_WIDTH = N_HEADS * V_HEAD_DIM
ROPE_THETA = 10000.0
ATTN_SCALE = 1.0 / math.sqrt(QK_NOPE_DIM + QK_ROPE_DIM)
Q_SCALE = ATTN_SCALE * math.log2(math.e)
NORM_EPS = 1e-6
MASK_VALUE = -1e30

OFF_U = 0
OFF_QK = POOL_WIDTH
OFF_KV = OFF_QK + Q_LORA_RANK + HEAD_PAD
OFF_GP = OFF_KV + KV_LORA_RANK
OFF_GM = OFF_GP + D_MODEL
IN_WIDTH_PAD = OFF_GM + D_MODEL

TM_FFN = 512
TM_MIX = 512
TQ = 256
TK = 256
KLEN_STEP = 256
S_SLOTS = 4
FF_CHUNK = 256
ADA_BLOCK = 1152
VMEM_LIMIT = 56 * 1024 * 1024


def _dot(a, b):
    return jnp.dot(a, b, preferred_element_type=F32)


def _rms_norm(x, g):
    return x * lax.rsqrt(jnp.mean(x * x, axis=-1, keepdims=True) + NORM_EPS) * g


def _norm_mod(x, g, shift, scale):
    return _rms_norm(x, g) * (1.0 + scale) + shift


def _ada_kernel(c_ref, w_ref, b_ref, o_ref):
    c = c_ref[...]
    c_act = (c * jax.nn.sigmoid(c)).astype(BF16)
    o_ref[...] = _dot(c_act, w_ref[...].astype(BF16)) + b_ref[...]


def _ada(c, w_ada, b_ada):
    batch = c.shape[0]
    n = w_ada.shape[1]
    return pl.pallas_call(
        _ada_kernel,
        grid=(n // ADA_BLOCK,),
        in_specs=[
            pl.BlockSpec((batch, D_MODEL), lambda i: (0, 0)),
            pl.BlockSpec((D_MODEL, ADA_BLOCK), lambda i: (0, i)),
            pl.BlockSpec((1, ADA_BLOCK), lambda i: (0, i)),
        ],
        out_specs=pl.BlockSpec((batch, ADA_BLOCK), lambda i: (0, i)),
        out_shape=jax.ShapeDtypeStruct((batch, n), F32),
        compiler_params=pltpu.CompilerParams(
            dimension_semantics=("arbitrary",), vmem_limit_bytes=VMEM_LIMIT),
        name="ada",
    )(c, w_ada, b_ada.reshape(1, n))


def _ffn_kernel(x_ref, mod_ref, g_ref, win_ref, wout_ref, o_ref, act_ref, *, sub):
    x = x_ref[...]
    mod = mod_ref[...]
    shift, scale, gate = mod[3 * sub:3 * sub + 1], mod[3 * sub + 1:3 * sub + 2], mod[3 * sub + 2:3 * sub + 3]
    h = _norm_mod(x, g_ref[...], shift, scale).astype(BF16)
    for c in range(D_FF // FF_CHUNK):
        lo = c * FF_CHUNK
        g = _dot(h, win_ref[:, lo:lo + FF_CHUNK])
        u = _dot(h, win_ref[:, D_FF + lo:D_FF + lo + FF_CHUNK])
        act_ref[:, lo:lo + FF_CHUNK] = (g * jax.nn.sigmoid(g) * u).astype(BF16)
    y = _dot(act_ref[...], wout_ref[...])
    o_ref[...] = x + (0.5 * gate) * y


def _const_spec(shape):
    return pl.BlockSpec(shape, lambda *_: (0,) * len(shape), pipeline_mode=pl.Buffered(1))


def _ffn(x2d, mod, g, w_in, w_out, *, sub, seq):
    tokens = x2d.shape[0]
    tiles_per_seq = seq // TM_FFN
    return pl.pallas_call(
        functools.partial(_ffn_kernel, sub=sub),
        grid=(tokens // TM_FFN,),
        in_specs=[
            pl.BlockSpec((TM_FFN, D_MODEL), lambda i: (i, 0)),
            pl.BlockSpec((pl.Squeezed(), N_MOD_ROWS, D_MODEL), lambda i: (i // tiles_per_seq, 0, 0)),
            _const_spec((1, D_MODEL)),
            _const_spec((D_MODEL, 2 * D_FF)),
            _const_spec((D_FF, D_MODEL)),
        ],
        out_specs=pl.BlockSpec((TM_FFN, D_MODEL), lambda i: (i, 0)),
        out_shape=jax.ShapeDtypeStruct((tokens, D_MODEL), F32),
        scratch_shapes=[pltpu.VMEM((TM_FFN, D_FF), BF16)],
        compiler_params=pltpu.CompilerParams(
            dimension_semantics=("arbitrary",), vmem_limit_bytes=VMEM_LIMIT),
        name=f"ffn{sub}",
    )(x2d, mod, g, w_in, w_out)


def _segment_mean_sq(z, seg):
    sq = z * z
    hi = sq.astype(BF16)
    lo = (sq - hi.astype(F32)).astype(BF16)
    width = seg.shape[0]
    outs = []
    for p in range(z.shape[1] // width):
        sl = slice(p * width, (p + 1) * width)
        outs.append(_dot(hi[:, sl], seg) + _dot(lo[:, sl], seg))
    return jnp.concatenate(outs, axis=1)


def _rope_head(z, cos_t, sin_lo, sin_hi):
    up = pltpu.roll(z, HEAD_PAD - QK_ROPE_DIM // 2, 1)
    down = pltpu.roll(z, QK_ROPE_DIM // 2, 1)
    return z * cos_t + up * sin_lo + down * sin_hi


def _mix_in_kernel(x_ref, mod_ref, pos_ref, gmix_ref, win_ref, pgrp_ref, pscale_ref, wpp_ref,
                   qan_ref, wq_ref, kvan_ref, wkv_ref, qg_ref, kg_ref, krg_ref, seg_ref, freq_ref,
                   q_out, kt_out, v_out, p_out, g_out, ext_ref, *, tiles_per_seq):
    tm = x_ref.shape[0]
    tile_in_seq = pl.program_id(0) % tiles_per_seq
    x = x_ref[...]
    mod = mod_ref[...]
    h = _norm_mod(x, gmix_ref[...], mod[3:4], mod[4:5]).astype(BF16)

    u = _dot(h, win_ref[:, OFF_U:OFF_U + POOL_WIDTH])

    @pl.when(tile_in_seq == 0)
    def _():
        ext_ref[0:POOL_HALO, :] = jnp.zeros((POOL_HALO, POOL_WIDTH), F32)

    @pl.when(tile_in_seq != 0)
    def _():
        ext_ref[0:POOL_HALO, :] = ext_ref[tm:tm + POOL_HALO, :]

    ext_ref[POOL_HALO:POOL_HALO + tm, :] = u
    t_in_seq = tile_in_seq * tm + lax.broadcasted_iota(jnp.int32, (tm, 1), 0)
    pooled = []
    for grp, window in enumerate(POOL_WINDOWS):
        lanes = slice(grp * POOL_GROUP_DIM, (grp + 1) * POOL_GROUP_DIM)
        u_g = u[:, lanes]
        acc = u_g
        for back in range(1, window):
            acc = acc + ext_ref[POOL_HALO - back:POOL_HALO - back + tm, lanes]
        cnt = jnp.minimum(t_in_seq + 1, window).astype(F32)
        pooled.append(_dot((acc / cnt - u_g).astype(BF16), pgrp_ref[grp]))
    pooled = jnp.concatenate(pooled, axis=1) * pscale_ref[...]
    br_pool = _dot(pooled.astype(BF16), wpp_ref[...])
    g_pool = _dot(h, win_ref[:, OFF_GP:OFF_GP + D_MODEL])
    p_out[...] = jax.nn.sigmoid(g_pool) * br_pool
    g_mla = _dot(h, win_ref[:, OFF_GM:OFF_GM + D_MODEL])
    g_out[...] = jax.nn.sigmoid(g_mla)

    seg = seg_ref[...]
    lane = lax.broadcasted_iota(jnp.int32, (1, HEAD_PAD), 1)
    ang = pos_ref[...].astype(F32) * freq_ref[...]
    cos_a, sin_a = jnp.cos(ang), jnp.sin(ang)
    half = QK_ROPE_DIM // 2
    cos_t = jnp.where(lane < QK_NOPE_DIM, 1.0, cos_a)
    sin_lo = jnp.where((lane >= QK_NOPE_DIM) & (lane < QK_NOPE_DIM + half), -sin_a, 0.0)
    sin_hi = jnp.where((lane >= QK_NOPE_DIM + half) & (lane < QK_NOPE_DIM + 2 * half), sin_a, 0.0)

    qk = _dot(h, win_ref[:, OFF_QK:OFF_QK + Q_LORA_RANK + HEAD_PAD])
    q_lat, k_rope = qk[:, :Q_LORA_RANK], qk[:, Q_LORA_RANK:]
    q = _dot(_rms_norm(q_lat, qan_ref[...]).astype(BF16), wq_ref[...])
    qn = q * lax.rsqrt(_segment_mean_sq(q, seg) + NORM_EPS) * qg_ref[...]
    for hd in range(N_HEADS):
        lanes = slice(hd * HEAD_PAD, (hd + 1) * HEAD_PAD)
        q_out[:, lanes] = (_rope_head(qn[:, lanes], cos_t, sin_lo, sin_hi) * Q_SCALE).astype(BF16)

    kv_lat = _dot(h, win_ref[:, OFF_KV:OFF_KV + KV_LORA_RANK])
    kvn = _rms_norm(kv_lat, kvan_ref[...]).astype(BF16)
    k_nope = _dot(kvn, wkv_ref[:, 0:N_HEADS * HEAD_PAD])
    v_out[...] = _dot(kvn, wkv_ref[:, N_HEADS * HEAD_PAD:]).astype(BF16)
    kn = k_nope * lax.rsqrt(_segment_mean_sq(k_nope, seg) + NORM_EPS) * kg_ref[...]
    kr_ms = jnp.sum(k_rope * k_rope, axis=-1, keepdims=True) * (1.0 / QK_ROPE_DIM)
    kr = _rope_head(k_rope * lax.rsqrt(kr_ms + NORM_EPS) * krg_ref[...], cos_t, sin_lo, sin_hi)
    for hd in range(N_HEADS):
        lanes = slice(hd * HEAD_PAD, (hd + 1) * HEAD_PAD)
        kt_out[lanes, :] = (kn[:, lanes] + kr).T.astype(BF16)


def _mix_in(x2d, mod, pos2d, gmix, w, *, seq):
    tokens = x2d.shape[0]
    tm = TM_MIX
    tiles_per_seq = seq // tm
    tok_spec = lambda width: pl.BlockSpec((tm, width), lambda i: (i, 0))
    consts = [gmix, w["w_in"], w["pool_grp"], w["pool_scale"], w["w_pool_proj"], w["q_a_norm"], w["w_q"],
              w["kv_a_norm"], w["w_kv"], w["q_gain"], w["k_gain"], w["kr_gain"], w["seg"], w["freq"]]
    return pl.pallas_call(
        functools.partial(_mix_in_kernel, tiles_per_seq=tiles_per_seq),
        grid=(tokens // tm,),
        in_specs=[
            tok_spec(D_MODEL),
            pl.BlockSpec((pl.Squeezed(), N_MOD_ROWS, D_MODEL), lambda i: (i // tiles_per_seq, 0, 0)),
            tok_spec(1),
        ] + [_const_spec(a.shape) for a in consts],
        out_specs=[tok_spec(N_HEADS * HEAD_PAD),
                   pl.BlockSpec((pl.Squeezed(), N_HEADS * HEAD_PAD, tm),
                                lambda i: (i // tiles_per_seq, 0, i % tiles_per_seq)),
                   tok_spec(MLA_WIDTH), tok_spec(D_MODEL), tok_spec(D_MODEL)],
        out_shape=[
            jax.ShapeDtypeStruct((tokens, N_HEADS * HEAD_PAD), BF16),
            jax.ShapeDtypeStruct((tokens // seq, N_HEADS * HEAD_PAD, seq), BF16),
            jax.ShapeDtypeStruct((tokens, MLA_WIDTH), BF16),
            jax.ShapeDtypeStruct((tokens, D_MODEL), F32),
            jax.ShapeDtypeStruct((tokens, D_MODEL), F32),
        ],
        scratch_shapes=[pltpu.VMEM((POOL_HALO + tm, POOL_WIDTH), F32)],
        compiler_params=pltpu.CompilerParams(
            dimension_semantics=("arbitrary",), vmem_limit_bytes=VMEM_LIMIT),
        name="mix_in",
    )(x2d, mod, pos2d, *consts)


def _attn_kernel(q_ref, kt_ref, v_ref, p_ref, g_ref, x_ref, mod_ref, wmla_ref, wout_ref, o_ref,
                 s_ref, e_ref, attn_ref):
    j = pl.program_id(1)
    seq = kt_ref.shape[1]
    row = lax.broadcasted_iota(jnp.int32, (TQ, TK), 0)
    col = lax.broadcasted_iota(jnp.int32, (TQ, TK), 1)
    lane = lax.broadcasted_iota(jnp.int32, (1, 2 * V_HEAD_DIM), 1)

    def lane_fold(s, op):
        out = s[:, 0:128]
        for t in range(1, TK // 128):
            out = op(out, s[:, t * 128:(t + 1) * 128])
        return out

    def score_pass(hd, klen):
        slot = hd % S_SLOTS
        rows = slice(hd * HEAD_PAD, (hd + 1) * HEAD_PAD)
        q_h = q_ref[:, rows]
        m_run = None
        for c in range(klen // TK):
            cols = slice(c * TK, (c + 1) * TK)
            s = _dot(q_h, kt_ref[rows, cols])
            if (c + 1) * TK > klen - KLEN_STEP:
                s = jnp.where(col + (c * TK - j * TQ) <= row, s, MASK_VALUE)
            s_ref[slot, :, cols] = s
            fold = lane_fold(s, jnp.maximum)
            m_run = fold if m_run is None else jnp.maximum(m_run, fold)
        return jnp.max(m_run, axis=-1, keepdims=True)

    def value_pass(hd, klen, m):
        s_slot, e_slot = hd % S_SLOTS, hd % 2
        l_run = None
        for c in range(klen // TK):
            cols = slice(c * TK, (c + 1) * TK)
            e = jnp.exp2(s_ref[s_slot, :, cols] - m)
            e_ref[e_slot, :, cols] = e.astype(BF16)
            fold = lane_fold(e, jnp.add)
            l_run = fold if l_run is None else l_run + fold
        pair = hd // 2
        acc = _dot(e_ref[e_slot, :, 0:klen], v_ref[0:klen, pair * 128:(pair + 1) * 128])
        return acc / jnp.sum(l_run, axis=-1, keepdims=True)

    def all_heads(klen):
        row_max, out = {}, {}
        for hd in range(N_HEADS + S_SLOTS):
            done = hd - S_SLOTS
            if done >= 0:
                out[done] = value_pass(done, klen, row_max.pop(done))
                if done % 2 == 1:
                    both = jnp.where(lane < V_HEAD_DIM, out.pop(done - 1), out.pop(done))
                    attn_ref[:, (done // 2) * 128:(done // 2 + 1) * 128] = both.astype(BF16)
            if hd < N_HEADS:
                row_max[hd] = score_pass(hd, klen)

    for variant in range(seq // KLEN_STEP):
        @pl.when((j * TQ) // KLEN_STEP == variant)
        def _(variant=variant):
            all_heads((variant + 1) * KLEN_STEP)

    br_mla = _dot(attn_ref[...], wmla_ref[...])
    merged = p_ref[...] + g_ref[...] * br_mla
    gate = mod_ref[...][5:6]
    o_ref[...] = x_ref[...] + gate * _dot(merged.astype(BF16), wout_ref[...])


def _attn(q, kt, v, p, g, x2d, mod, w_mla, w_out, *, batch, seq):
    tokens = x2d.shape[0]
    nq = seq // TQ
    tok_spec = lambda width: pl.BlockSpec((TQ, width), lambda b, j: (b * nq + j, 0))
    return pl.pallas_call(
        _attn_kernel,
        grid=(batch, nq),
        in_specs=[
            tok_spec(N_HEADS * HEAD_PAD),
            pl.BlockSpec((pl.Squeezed(), N_HEADS * HEAD_PAD, seq), lambda b, j: (b, 0, 0)),
            pl.BlockSpec((seq, MLA_WIDTH), lambda b, j: (b, 0)),
            tok_spec(D_MODEL), tok_spec(D_MODEL), tok_spec(D_MODEL),
            pl.BlockSpec((pl.Squeezed(), N_MOD_ROWS, D_MODEL), lambda b, j: (b, 0, 0)),
            _const_spec((MLA_WIDTH, D_MODEL)), _const_spec((D_MODEL, D_MODEL)),
        ],
        out_specs=tok_spec(D_MODEL),
        out_shape=jax.ShapeDtypeStruct((tokens, D_MODEL), F32),
        scratch_shapes=[pltpu.VMEM((S_SLOTS, TQ, seq), F32), pltpu.VMEM((2, TQ, seq), BF16),
                        pltpu.VMEM((TQ, MLA_WIDTH), BF16)],
        compiler_params=pltpu.CompilerParams(
            dimension_semantics=("arbitrary", "arbitrary"), vmem_limit_bytes=VMEM_LIMIT),
        name="attn",
    )(q, kt, v, p, g, x2d, mod, w_mla, w_out)


def _prep_mixer_weights(w_in, pool_grp, pool_scale, w_pool_proj, q_a_norm, w_q_up, kv_a_norm, w_kv_up,
                        q_norm_nope, q_norm_rope, k_norm_nope, k_norm_rope):
    splits = np.cumsum([POOL_WIDTH, Q_LORA_RANK, KV_LORA_RANK, QK_ROPE_DIM, D_MODEL])
    u_w, q_w, kv_w, kr_w, gp_w, gm_w = jnp.split(w_in, splits, axis=1)
    kr_w = jnp.pad(kr_w, ((0, 0), (QK_NOPE_DIM, HEAD_PAD - QK_NOPE_DIM - QK_ROPE_DIM)))
    w_in_p = jnp.concatenate([u_w, q_w, kr_w, kv_w, gp_w, gm_w], axis=1).astype(BF16)

    qk_head = QK_NOPE_DIM + QK_ROPE_DIM
    w_q = jnp.pad(w_q_up.reshape(Q_LORA_RANK, N_HEADS, qk_head), ((0, 0), (0, 0), (0, HEAD_PAD - qk_head)))
    w_q = w_q.reshape(Q_LORA_RANK, N_HEADS * HEAD_PAD).astype(BF16)
    kv = w_kv_up.reshape(KV_LORA_RANK, N_HEADS, QK_NOPE_DIM + V_HEAD_DIM)
    w_k = jnp.pad(kv[..., :QK_NOPE_DIM], ((0, 0), (0, 0), (0, HEAD_PAD - QK_NOPE_DIM)))
    w_kv = jnp.concatenate([w_k.reshape(KV_LORA_RANK, N_HEADS * HEAD_PAD),
                            kv[..., QK_NOPE_DIM:].reshape(KV_LORA_RANK, MLA_WIDTH)], axis=1).astype(BF16)

    zeros = lambda n: jnp.zeros((n,), F32)
    q_gain = jnp.tile(jnp.concatenate([q_norm_nope, q_norm_rope, zeros(HEAD_PAD - qk_head)]), N_HEADS)
    k_gain = jnp.tile(jnp.concatenate([k_norm_nope, zeros(HEAD_PAD - QK_NOPE_DIM)]), N_HEADS)
    kr_gain = jnp.concatenate([zeros(QK_NOPE_DIM), k_norm_rope, zeros(HEAD_PAD - qk_head)])

    seg = np.zeros((2 * HEAD_PAD, 2 * HEAD_PAD), np.float32)
    for base in (0, HEAD_PAD):
        seg[base:base + QK_NOPE_DIM, base:base + QK_NOPE_DIM] = 1.0 / QK_NOPE_DIM
        seg[base + QK_NOPE_DIM:base + qk_head, base + QK_NOPE_DIM:base + qk_head] = 1.0 / QK_ROPE_DIM

    inv_freq = ROPE_THETA ** (-jnp.arange(0, QK_ROPE_DIM, 2, dtype=F32) / QK_ROPE_DIM)
    freq = jnp.concatenate([zeros(QK_NOPE_DIM), inv_freq, inv_freq, zeros(HEAD_PAD - qk_head)])

    return {
        "w_in": w_in_p, "pool_grp": pool_grp.astype(BF16), "pool_scale": pool_scale.reshape(1, POOL_WIDTH),
        "w_pool_proj": w_pool_proj.astype(BF16), "q_a_norm": q_a_norm.reshape(1, Q_LORA_RANK), "w_q": w_q,
        "kv_a_norm": kv_a_norm.reshape(1, KV_LORA_RANK), "w_kv": w_kv,
        "q_gain": q_gain.reshape(1, -1), "k_gain": k_gain.reshape(1, -1), "kr_gain": kr_gain.reshape(1, -1),
        "seg": jnp.asarray(seg, BF16), "freq": freq.reshape(1, HEAD_PAD),
    }


def kernel(x, c, positions, w_ada, b_ada, norm_ffn1, w_ffn1_in, w_ffn1_out, norm_mix, w_in, pool_grp,
           pool_scale, w_pool_proj, q_a_norm, w_q_up, kv_a_norm, w_kv_up, q_norm_nope, q_norm_rope,
           k_norm_nope, k_norm_rope, w_mla_proj, w_out, norm_ffn2, w_ffn2_in, w_ffn2_out):
    batch, seq, d = x.shape
    depth = w_ada.shape[0]
    assert d == D_MODEL and seq % TM_FFN == 0 and seq % TM_MIX == 0
    assert seq % KLEN_STEP == 0 and KLEN_STEP % TQ == 0 and KLEN_STEP % TK == 0
    x2d = x.reshape(batch * seq, d)
    pos2d = positions.reshape(batch * seq, 1)
    for l in range(depth):
        mod = _ada(c, w_ada[l], b_ada[l]).reshape(batch, N_MOD_ROWS, d)
        x2d = _ffn(x2d, mod, norm_ffn1[l].reshape(1, d), w_ffn1_in[l].astype(BF16),
                   w_ffn1_out[l].astype(BF16), sub=0, seq=seq)
        w = _prep_mixer_weights(w_in[l], pool_grp[l], pool_scale[l], w_pool_proj[l], q_a_norm[l], w_q_up[l],
                                kv_a_norm[l], w_kv_up[l], q_norm_nope[l], q_norm_rope[l], k_norm_nope[l],
                                k_norm_rope[l])
        q, kt, v, p, g = _mix_in(x2d, mod, pos2d, norm_mix[l].reshape(1, d), w, seq=seq)
        x2d = _attn(q, kt, v, p, g, x2d, mod, w_mla_proj[l].astype(BF16), w_out[l].astype(BF16),
                    batch=batch, seq=seq)
        x2d = _ffn(x2d, mod, norm_ffn2[l].reshape(1, d), w_ffn2_in[l].astype(BF16),
                   w_ffn2_out[l].astype(BF16), sub=2, seq=seq)
    return x2d.reshape(batch, seq, d)
```

```python
import functools
import math

import numpy as np
import jax
import jax.numpy as jnp
from jax import lax
from jax.experimental import pallas as pl
from jax.experimental.pallas import tpu as pltpu

F32 = jnp.float32
BF16 = jnp.bfloat16

D_MODEL = 1024
D_FF = 2816
N_MOD_ROWS = 9
POOL_WINDOWS = (2, 4, 8, 16)
POOL_WIDTH = 512
POOL_GROUP_DIM = 128
POOL_HALO = 16
N_HEADS = 8
QK_NOPE_DIM = 64
QK_ROPE_DIM = 32
V_HEAD_DIM = 64
HEAD_PAD = 128
Q_LORA_RANK = 384
KV_LORA_RANK = 256
MLA_WIDTH = N_HEADS * V_HEAD_DIM
ROPE_THETA = 10000.0
ATTN_SCALE = 1.0 / math.sqrt(QK_NOPE_DIM + QK_ROPE_DIM)
Q_SCALE = ATTN_SCALE * math.log2(math.e)
NORM_EPS = 1e-6
MASK_VALUE = -1e30

OFF_U = 0
OFF_QK = POOL_WIDTH
OFF_KV = OFF_QK + Q_LORA_RANK + HEAD_PAD
OFF_GP = OFF_KV + KV_LORA_RANK
OFF_GM = OFF_GP + D_MODEL
IN_WIDTH_PAD = OFF_GM + D_MODEL

TM_FFN = 512
TM_MIX = 512
TQ = 256
TK = 256
S_SLOTS = 4
FF_CHUNK = 256
ADA_BLOCK = 1152
VMEM_LIMIT = 56 * 1024 * 1024


def _dot(a, b):
    return jnp.dot(a, b, preferred_element_type=F32)


def _rms_norm(x, g):
    return x * lax.rsqrt(jnp.mean(x * x, axis=-1, keepdims=True) + NORM_EPS) * g


def _norm_mod(x, g, shift, scale):
    return _rms_norm(x, g) * (1.0 + scale) + shift


def _ada_kernel(c_ref, w_ref, b_ref, o_ref):
    c = c_ref[...]
    c_act = (c * jax.nn.sigmoid(c)).astype(BF16)
    o_ref[...] = _dot(c_act, w_ref[...].astype(BF16)) + b_ref[...]


def _ada(c, w_ada, b_ada):
    batch = c.shape[0]
    n = w_ada.shape[1]
    return pl.pallas_call(
        _ada_kernel,
        grid=(n // ADA_BLOCK,),
        in_specs=[
            pl.BlockSpec((batch, D_MODEL), lambda i: (0, 0)),
            pl.BlockSpec((D_MODEL, ADA_BLOCK), lambda i: (0, i)),
            pl.BlockSpec((1, ADA_BLOCK), lambda i: (0, i)),
        ],
        out_specs=pl.BlockSpec((batch, ADA_BLOCK), lambda i: (0, i)),
        out_shape=jax.ShapeDtypeStruct((batch, n), F32),
        compiler_params=pltpu.CompilerParams(
            dimension_semantics=("arbitrary",), vmem_limit_bytes=VMEM_LIMIT),
        name="ada",
    )(c, w_ada, b_ada.reshape(1, n))


def _ffn_kernel(x_ref, mod_ref, g_ref, win_ref, wout_ref, o_ref, act_ref, *, sub):
    x = x_ref[...]
    mod = mod_ref[...]
    shift, scale, gate = mod[3 * sub:3 * sub + 1], mod[3 * sub + 1:3 * sub + 2], mod[3 * sub + 2:3 * sub + 3]
    h = _norm_mod(x, g_ref[...], shift, scale).astype(BF16)
    for c in range(D_FF // FF_CHUNK):
        lo = c * FF_CHUNK
        g = _dot(h, win_ref[:, lo:lo + FF_CHUNK])
        u = _dot(h, win_ref[:, D_FF + lo:D_FF + lo + FF_CHUNK])
        act_ref[:, lo:lo + FF_CHUNK] = (g * jax.nn.sigmoid(g) * u).astype(BF16)
    y = _dot(act_ref[...], wout_ref[...])
    o_ref[...] = x + (0.5 * gate) * y


def _const_spec(shape):
    return pl.BlockSpec(shape, lambda *_: (0,) * len(shape), pipeline_mode=pl.Buffered(1))


def _ffn(x2d, mod, g, w_in, w_out, *, sub, seq):
    tokens = x2d.shape[0]
    tiles_per_seq = seq // TM_FFN
    return pl.pallas_call(
        functools.partial(_ffn_kernel, sub=sub),
        grid=(tokens // TM_FFN,),
        in_specs=[
            pl.BlockSpec((TM_FFN, D_MODEL), lambda i: (i, 0)),
            pl.BlockSpec((pl.Squeezed(), N_MOD_ROWS, D_MODEL), lambda i: (i // tiles_per_seq, 0, 0)),
            _const_spec((1, D_MODEL)),
            _const_spec((D_MODEL, 2 * D_FF)),
            _const_spec((D_FF, D_MODEL)),
        ],
        out_specs=pl.BlockSpec((TM_FFN, D_MODEL), lambda i: (i, 0)),
        out_shape=jax.ShapeDtypeStruct((tokens, D_MODEL), F32),
        scratch_shapes=[pltpu.VMEM((TM_FFN, D_FF), BF16)],
        compiler_params=pltpu.CompilerParams(
            dimension_semantics=("arbitrary",), vmem_limit_bytes=VMEM_LIMIT),
        name=f"ffn{sub}",
    )(x2d, mod, g, w_in, w_out)


def _segment_mean_sq(z, seg):
    sq = z * z
    hi = sq.astype(BF16)
    lo = (sq - hi.astype(F32)).astype(BF16)
    width = seg.shape[0]
    outs = []
    for p in range(z.shape[1] // width):
        sl = slice(p * width, (p + 1) * width)
        outs.append(_dot(hi[:, sl], seg) + _dot(lo[:, sl], seg))
    return jnp.concatenate(outs, axis=1)


def _rope_head(z, cos_t, sin_lo, sin_hi):
    up = pltpu.roll(z, HEAD_PAD - QK_ROPE_DIM // 2, 1)
    down = pltpu.roll(z, QK_ROPE_DIM // 2, 1)
    return z * cos_t + up * sin_lo + down * sin_hi


def _mix_in_kernel(x_ref, mod_ref, pos_ref, gmix_ref, win_ref, pgrp_ref, pscale_ref, wpp_ref,
                   qan_ref, wq_ref, kvan_ref, wkv_ref, qg_ref, kg_ref, krg_ref, seg_ref, freq_ref,
                   q_out, kt_out, v_out, p_out, g_out, ext_ref, *, tiles_per_seq):
    tm = x_ref.shape[0]
    tile_in_seq = pl.program_id(0) % tiles_per_seq
    pair_width = 2 * HEAD_PAD
    seg = seg_ref[...]
    x = x_ref[...]
    mod = mod_ref[...]
    h = _norm_mod(x, gmix_ref[...], mod[3:4], mod[4:5]).astype(BF16)

    qk = _dot(h, win_ref[:, OFF_QK:OFF_QK + Q_LORA_RANK + HEAD_PAD])
    q_lat, k_rope = qk[:, :Q_LORA_RANK], qk[:, Q_LORA_RANK:]
    kv_lat = _dot(h, win_ref[:, OFF_KV:OFF_KV + KV_LORA_RANK])
    u = _dot(h, win_ref[:, OFF_U:OFF_U + POOL_WIDTH])

    lane = lax.broadcasted_iota(jnp.int32, (1, HEAD_PAD), 1)
    ang = pos_ref[...].astype(F32) * freq_ref[...]
    cos_a, sin_a = jnp.cos(ang), jnp.sin(ang)
    half = QK_ROPE_DIM // 2
    cos_t = jnp.where(lane < QK_NOPE_DIM, 1.0, cos_a)
    sin_lo = jnp.where((lane >= QK_NOPE_DIM) & (lane < QK_NOPE_DIM + half), -sin_a, 0.0)
    sin_hi = jnp.where((lane >= QK_NOPE_DIM + half) & (lane < QK_NOPE_DIM + 2 * half), sin_a, 0.0)

    q = _dot(_rms_norm(q_lat, qan_ref[...]).astype(BF16), wq_ref[...])
    kvn = _rms_norm(kv_lat, kvan_ref[...]).astype(BF16)
    k_nope = _dot(kvn, wkv_ref[:, 0:N_HEADS * HEAD_PAD])
    v_out[...] = _dot(kvn, wkv_ref[:, N_HEADS * HEAD_PAD:]).astype(BF16)

    @pl.when(tile_in_seq == 0)
    def _():
        ext_ref[0:POOL_HALO, :] = jnp.zeros((POOL_HALO, POOL_WIDTH), F32)

    @pl.when(tile_in_seq != 0)
    def _():
        ext_ref[0:POOL_HALO, :] = ext_ref[tm:tm + POOL_HALO, :]

    ext_ref[POOL_HALO:POOL_HALO + tm, :] = u
    t_in_seq = tile_in_seq * tm + lax.broadcasted_iota(jnp.int32, (tm, 1), 0)
    pooled = []
    for grp, window in enumerate(POOL_WINDOWS):
        lanes = slice(grp * POOL_GROUP_DIM, (grp + 1) * POOL_GROUP_DIM)
        u_g = u[:, lanes]
        acc = u_g
        for back in range(1, window):
            acc = acc + ext_ref[POOL_HALO - back:POOL_HALO - back + tm, lanes]
        cnt = jnp.minimum(t_in_seq + 1, window).astype(F32)
        pooled.append(_dot((acc / cnt - u_g).astype(BF16), pgrp_ref[grp]))
    pooled = jnp.concatenate(pooled, axis=1) * pscale_ref[...]
    br_pool = _dot(pooled.astype(BF16), wpp_ref[...])

    kr_ms = jnp.sum(k_rope * k_rope, axis=-1, keepdims=True) * (1.0 / QK_ROPE_DIM)
    kr = _rope_head(k_rope * lax.rsqrt(kr_ms + NORM_EPS) * krg_ref[...], cos_t, sin_lo, sin_hi)

    assert D_MODEL == N_HEADS * HEAD_PAD
    for pair in range(N_HEADS // 2):
        lanes = slice(pair * pair_width, (pair + 1) * pair_width)
        g_pool = _dot(h, win_ref[:, OFF_GP + pair * pair_width:OFF_GP + (pair + 1) * pair_width])
        p_out[:, lanes] = jax.nn.sigmoid(g_pool) * br_pool[:, lanes]
        q_p = q[:, lanes]
        qn = q_p * lax.rsqrt(_segment_mean_sq(q_p, seg) + NORM_EPS) * qg_ref[:, lanes]
        for sub in range(2):
            hd = 2 * pair + sub
            roped = _rope_head(qn[:, sub * HEAD_PAD:(sub + 1) * HEAD_PAD], cos_t, sin_lo, sin_hi)
            q_out[:, hd * HEAD_PAD:(hd + 1) * HEAD_PAD] = (roped * Q_SCALE).astype(BF16)
        g_mla = _dot(h, win_ref[:, OFF_GM + pair * pair_width:OFF_GM + (pair + 1) * pair_width])
        g_out[:, lanes] = jax.nn.sigmoid(g_mla)
        k_p = k_nope[:, lanes]
        kn = k_p * lax.rsqrt(_segment_mean_sq(k_p, seg) + NORM_EPS) * kg_ref[:, lanes]
        for sub in range(2):
            hd = 2 * pair + sub
            k_h = kn[:, sub * HEAD_PAD:(sub + 1) * HEAD_PAD] + kr
            kt_out[hd * HEAD_PAD:(hd + 1) * HEAD_PAD, :] = k_h.T.astype(BF16)


def _mix_in(x2d, mod, pos2d, gmix, w, *, seq):
    tokens = x2d.shape[0]
    tm = TM_MIX
    tiles_per_seq = seq // tm
    tok_spec = lambda width: pl.BlockSpec((tm, width), lambda i: (i, 0))
    consts = [gmix, w["w_in"], w["pool_grp"], w["pool_scale"], w["w_pool_proj"], w["q_a_norm"], w["w_q"],
              w["kv_a_norm"], w["w_kv"], w["q_gain"], w["k_gain"], w["kr_gain"], w["seg"], w["freq"]]
    return pl.pallas_call(
        functools.partial(_mix_in_kernel, tiles_per_seq=tiles_per_seq),
        grid=(tokens // tm,),
        in_specs=[
            tok_spec(D_MODEL),
            pl.BlockSpec((pl.Squeezed(), N_MOD_ROWS, D_MODEL), lambda i: (i // tiles_per_seq, 0, 0)),
            tok_spec(1),
        ] + [_const_spec(a.shape) for a in consts],
        out_specs=[tok_spec(N_HEADS * HEAD_PAD),
                   pl.BlockSpec((pl.Squeezed(), N_HEADS * HEAD_PAD, tm),
                                lambda i: (i // tiles_per_seq, 0, i % tiles_per_seq)),
                   tok_spec(MLA_WIDTH), tok_spec(D_MODEL), tok_spec(D_MODEL)],
        out_shape=[
            jax.ShapeDtypeStruct((tokens, N_HEADS * HEAD_PAD), BF16),
            jax.ShapeDtypeStruct((tokens // seq, N_HEADS * HEAD_PAD, seq), BF16),
            jax.ShapeDtypeStruct((tokens, MLA_WIDTH), BF16),
            jax.ShapeDtypeStruct((tokens, D_MODEL), F32),
            jax.ShapeDtypeStruct((tokens, D_MODEL), F32),
        ],
        scratch_shapes=[pltpu.VMEM((POOL_HALO + tm, POOL_WIDTH), F32)],
        compiler_params=pltpu.CompilerParams(
            dimension_semantics=("arbitrary",), vmem_limit_bytes=VMEM_LIMIT),
        name="mix_in",
    )(x2d, mod, pos2d, *consts)


def _attn_kernel(q_ref, kt_ref, v_ref, p_ref, g_ref, x_ref, mod_ref, wmla_ref, wout_ref, o_ref,
                 s_ref, attn_ref):
    j = pl.program_id(1)
    seq = kt_ref.shape[1]
    row = lax.broadcasted_iota(jnp.int32, (TK, TK), 0)
    col = lax.broadcasted_iota(jnp.int32, (TK, TK), 1)
    causal = col <= row
    lane = lax.broadcasted_iota(jnp.int32, (1, 2 * V_HEAD_DIM), 1)

    def lane_fold(s, op):
        out = s[:, 0:128]
        for t in range(1, TK // 128):
            out = op(out, s[:, t * 128:(t + 1) * 128])
        return out

    def score_pass(unit, hd, block, klen):
        slot = unit % S_SLOTS
        lanes = slice(hd * HEAD_PAD, (hd + 1) * HEAD_PAD)
        q_h = q_ref[block * TK:(block + 1) * TK, lanes]
        n_chunks = klen // TK
        m_run = None
        for c in range(n_chunks):
            cols = slice(c * TK, (c + 1) * TK)
            s = _dot(q_h, kt_ref[lanes, cols])
            if c == n_chunks - 1:
                s = jnp.where(causal, s, MASK_VALUE)
            s_ref[slot, :, cols] = s
            fold = lane_fold(s, jnp.maximum)
            m_run = fold if m_run is None else jnp.maximum(m_run, fold)
        return jnp.max(m_run, axis=-1, keepdims=True)

    def value_pass(unit, hd, klen, m):
        slot = unit % S_SLOTS
        v_lanes = slice((hd // 2) * 128, (hd // 2 + 1) * 128)
        l_run = acc = None
        for c in range(klen // TK):
            cols = slice(c * TK, (c + 1) * TK)
            e = jnp.exp2(s_ref[slot, :, cols] - m)
            fold = lane_fold(e, jnp.add)
            l_run = fold if l_run is None else l_run + fold
            pv = _dot(e.astype(BF16), v_ref[cols, v_lanes])
            acc = pv if acc is None else acc + pv
        return acc / jnp.sum(l_run, axis=-1, keepdims=True)

    def all_units(first_key_end):
        units = [(2 * pair + sub, block) for pair in range(N_HEADS // 2) for block in range(TQ // TK)
                 for sub in range(2)]
        klen = lambda block: first_key_end + block * TK
        row_max, out = {}, {}
        for n in range(len(units) + S_SLOTS):
            done = n - S_SLOTS
            if done >= 0:
                hd, block = units[done]
                out[done] = value_pass(done, hd, klen(block), row_max.pop(done))
                if done % 2 == 1:
                    both = jnp.where(lane < V_HEAD_DIM, out.pop(done - 1), out.pop(done))
                    attn_ref[block * TK:(block + 1) * TK, (hd // 2) * 128:(hd // 2 + 1) * 128] = both.astype(BF16)
            if n < len(units):
                hd, block = units[n]
                row_max[n] = score_pass(n, hd, block, klen(block))

    for variant in range(seq // TQ):
        @pl.when(j == variant)
        def _(variant=variant):
            all_units(variant * TQ + TK)

    br_mla = _dot(attn_ref[...], wmla_ref[...])
    merged = p_ref[...] + g_ref[...] * br_mla
    gate = mod_ref[...][5:6]
    o_ref[...] = x_ref[...] + gate * _dot(merged.astype(BF16), wout_ref[...])


def _attn(q, kt, v, p, g, x2d, mod, w_mla, w_out, *, batch, seq):
    tokens = x2d.shape[0]
    nq = seq // TQ
    tok_spec = lambda width: pl.BlockSpec((TQ, width), lambda b, j: (b * nq + j, 0))
    return pl.pallas_call(
        _attn_kernel,
        grid=(batch, nq),
        in_specs=[
            tok_spec(N_HEADS * HEAD_PAD),
            pl.BlockSpec((pl.Squeezed(), N_HEADS * HEAD_PAD, seq), lambda b, j: (b, 0, 0)),
            pl.BlockSpec((seq, MLA_WIDTH), lambda b, j: (b, 0)),
            tok_spec(D_MODEL), tok_spec(D_MODEL), tok_spec(D_MODEL),
            pl.BlockSpec((pl.Squeezed(), N_MOD_ROWS, D_MODEL), lambda b, j: (b, 0, 0)),
            _const_spec((MLA_WIDTH, D_MODEL)), _const_spec((D_MODEL, D_MODEL)),
        ],
        out_specs=tok_spec(D_MODEL),
        out_shape=jax.ShapeDtypeStruct((tokens, D_MODEL), F32),
        scratch_shapes=[pltpu.VMEM((S_SLOTS, TK, seq), F32), pltpu.VMEM((TQ, MLA_WIDTH), BF16)],
        compiler_params=pltpu.CompilerParams(
            dimension_semantics=("arbitrary", "arbitrary"), vmem_limit_bytes=VMEM_LIMIT),
        name="attn",
    )(q, kt, v, p, g, x2d, mod, w_mla, w_out)


def _prep_mixer_weights(w_in, pool_grp, pool_scale, w_pool_proj, q_a_norm, w_q_up, kv_a_norm, w_kv_up,
                        q_norm_nope, q_norm_rope, k_norm_nope, k_norm_rope):
    splits = np.cumsum([POOL_WIDTH, Q_LORA_RANK, KV_LORA_RANK, QK_ROPE_DIM, D_MODEL])
    u_w, q_w, kv_w, kr_w, gp_w, gm_w = jnp.split(w_in, splits, axis=1)
    kr_w = jnp.pad(kr_w, ((0, 0), (QK_NOPE_DIM, HEAD_PAD - QK_NOPE_DIM - QK_ROPE_DIM)))
    w_in_p = jnp.concatenate([u_w, q_w, kr_w, kv_w, gp_w, gm_w], axis=1).astype(BF16)

    qk_head = QK_NOPE_DIM + QK_ROPE_DIM
    w_q = jnp.pad(w_q_up.reshape(Q_LORA_RANK, N_HEADS, qk_head), ((0, 0), (0, 0), (0, HEAD_PAD - qk_head)))
    w_q = w_q.reshape(Q_LORA_RANK, N_HEADS * HEAD_PAD).astype(BF16)
    kv = w_kv_up.reshape(KV_LORA_RANK, N_HEADS, QK_NOPE_DIM + V_HEAD_DIM)
    w_k = jnp.pad(kv[..., :QK_NOPE_DIM], ((0, 0), (0, 0), (0, HEAD_PAD - QK_NOPE_DIM)))
    w_kv = jnp.concatenate([w_k.reshape(KV_LORA_RANK, N_HEADS * HEAD_PAD),
                            kv[..., QK_NOPE_DIM:].reshape(KV_LORA_RANK, MLA_WIDTH)], axis=1).astype(BF16)

    zeros = lambda n: jnp.zeros((n,), F32)
    q_gain = jnp.tile(jnp.concatenate([q_norm_nope, q_norm_rope, zeros(HEAD_PAD - qk_head)]), N_HEADS)
    k_gain = jnp.tile(jnp.concatenate([k_norm_nope, zeros(HEAD_PAD - QK_NOPE_DIM)]), N_HEADS)
    kr_gain = jnp.concatenate([zeros(QK_NOPE_DIM), k_norm_rope, zeros(HEAD_PAD - qk_head)])

    seg = np.zeros((2 * HEAD_PAD, 2 * HEAD_PAD), np.float32)
    for base in (0, HEAD_PAD):
        seg[base:base + QK_NOPE_DIM, base:base + QK_NOPE_DIM] = 1.0 / QK_NOPE_DIM
        seg[base + QK_NOPE_DIM:base + qk_head, base + QK_NOPE_DIM:base + qk_head] = 1.0 / QK_ROPE_DIM

    inv_freq = ROPE_THETA ** (-jnp.arange(0, QK_ROPE_DIM, 2, dtype=F32) / QK_ROPE_DIM)
    freq = jnp.concatenate([zeros(QK_NOPE_DIM), inv_freq, inv_freq, zeros(HEAD_PAD - qk_head)])

    return {
        "w_in": w_in_p, "pool_grp": pool_grp.astype(BF16), "pool_scale": pool_scale.reshape(1, POOL_WIDTH),
        "w_pool_proj": w_pool_proj.astype(BF16), "q_a_norm": q_a_norm.reshape(1, Q_LORA_RANK), "w_q": w_q,
        "kv_a_norm": kv_a_norm.reshape(1, KV_LORA_RANK), "w_kv": w_kv,
        "q_gain": q_gain.reshape(1, -1), "k_gain": k_gain.reshape(1, -1), "kr_gain": kr_gain.reshape(1, -1),
        "seg": jnp.asarray(seg, BF16), "freq": freq.reshape(1, HEAD_PAD),
    }


def kernel(x, c, positions, w_ada, b_ada, norm_ffn1, w_ffn1_in, w_ffn1_out, norm_mix, w_in, pool_grp,
           pool_scale, w_pool_proj, q_a_norm, w_q_up, kv_a_norm, w_kv_up, q_norm_nope, q_norm_rope,
           k_norm_nope, k_norm_rope, w_mla_proj, w_out, norm_ffn2, w_ffn2_in, w_ffn2_out):
    batch, seq, d = x.shape
    depth = w_ada.shape[0]
    assert d == D_MODEL and seq % TM_FFN == 0 and seq % TM_MIX == 0
    assert seq % TQ == 0 and TQ % TK == 0
    x2d = x.reshape(batch * seq, d)
    pos2d = positions.reshape(batch * seq, 1)
    for l in range(depth):
        mod = _ada(c, w_ada[l], b_ada[l]).reshape(batch, N_MOD_ROWS, d)
        x2d = _ffn(x2d, mod, norm_ffn1[l].reshape(1, d), w_ffn1_in[l].astype(BF16),
                   w_ffn1_out[l].astype(BF16), sub=0, seq=seq)
        w = _prep_mixer_weights(w_in[l], pool_grp[l], pool_scale[l], w_pool_proj[l], q_a_norm[l], w_q_up[l],
                                kv_a_norm[l], w_kv_up[l], q_norm_nope[l], q_norm_rope[l], k_norm_nope[l],
                                k_norm_rope[l])
        q, kt, v, p, g = _mix_in(x2d, mod, pos2d, norm_mix[l].reshape(1, d), w, seq=seq)
        x2d = _attn(q, kt, v, p, g, x2d, mod, w_mla_proj[l].astype(BF16), w_out[l].astype(BF16),
                    batch=batch, seq=seq)
        x2d = _ffn(x2d, mod, norm_ffn2[l].reshape(1, d), w_ffn2_in[l].astype(BF16),
                   w_ffn2_out[l].astype(BF16), sub=2, seq=seq)
    return x2d.reshape(batch, seq, d)
```

```python
import functools
import math

import numpy as np
import jax
import jax.numpy as jnp
from jax import lax
from jax.experimental import pallas as pl
from jax.experimental.pallas import tpu as pltpu

F32 = jnp.float32
BF16 = jnp.bfloat16

D_MODEL = 1024
D_FF = 2816
N_MOD_ROWS = 9
POOL_WINDOWS = (2, 4, 8, 16)
POOL_WIDTH = 512
POOL_GROUP_DIM = 128
POOL_HALO = 16
N_HEADS = 8
QK_NOPE_DIM = 64
QK_ROPE_DIM = 32
V_HEAD_DIM = 64
HEAD_PAD = 128
Q_LORA_RANK = 384
KV_LORA_RANK = 256
MLA_WIDTH = N_HEADS * V_HEAD_DIM
ROPE_THETA = 10000.0
ATTN_SCALE = 1.0 / math.sqrt(QK_NOPE_DIM + QK_ROPE_DIM)
Q_SCALE = ATTN_SCALE * math.log2(math.e)
NORM_EPS = 1e-6
MASK_VALUE = -1e30

OFF_U = 0
OFF_QK = POOL_WIDTH
OFF_KV = OFF_QK + Q_LORA_RANK + HEAD_PAD
OFF_GP = OFF_KV + KV_LORA_RANK
OFF_GM = OFF_GP + D_MODEL
IN_WIDTH_PAD = OFF_GM + D_MODEL

TM_FFN = 512
TM_MIX = 512
TQ = 256
TK = 256
S_SLOTS = 4
FF_CHUNK = 256
ADA_BLOCK = 1152
VMEM_LIMIT = 56 * 1024 * 1024


def _dot(a, b):
    return jnp.dot(a, b, preferred_element_type=F32)


def _rms_norm(x, g):
    return x * lax.rsqrt(jnp.mean(x * x, axis=-1, keepdims=True) + NORM_EPS) * g


def _norm_mod(x, g, shift, scale):
    return _rms_norm(x, g) * (1.0 + scale) + shift


def _ada_kernel(c_ref, w_ref, b_ref, o_ref):
    c = c_ref[...]
    c_act = (c * jax.nn.sigmoid(c)).astype(BF16)
    o_ref[...] = _dot(c_act, w_ref[...].astype(BF16)) + b_ref[...]


def _ada(c, w_ada, b_ada):
    batch = c.shape[0]
    n = w_ada.shape[1]
    return pl.pallas_call(
        _ada_kernel,
        grid=(n // ADA_BLOCK,),
        in_specs=[
            pl.BlockSpec((batch, D_MODEL), lambda i: (0, 0)),
            pl.BlockSpec((D_MODEL, ADA_BLOCK), lambda i: (0, i)),
            pl.BlockSpec((1, ADA_BLOCK), lambda i: (0, i)),
        ],
        out_specs=pl.BlockSpec((batch, ADA_BLOCK), lambda i: (0, i)),
        out_shape=jax.ShapeDtypeStruct((batch, n), F32),
        compiler_params=pltpu.CompilerParams(
            dimension_semantics=("arbitrary",), vmem_limit_bytes=VMEM_LIMIT),
        name="ada",
    )(c, w_ada, b_ada.reshape(1, n))


def _ffn_kernel(x_ref, mod_ref, xn_ref, modn_ref, g_ref, win_ref, wout_ref, o_ref, h_ref, act_ref, *, sub):
    def normed(x, mod):
        return _norm_mod(x, g_ref[...], mod[3 * sub:3 * sub + 1], mod[3 * sub + 1:3 * sub + 2]).astype(BF16)

    def hidden_chunk(c):
        lo = c * FF_CHUNK
        g = _dot(h_ref[...], win_ref[:, lo:lo + FF_CHUNK])
        u = _dot(h_ref[...], win_ref[:, D_FF + lo:D_FF + lo + FF_CHUNK])
        act_ref[:, lo:lo + FF_CHUNK] = (g * jax.nn.sigmoid(g) * u).astype(BF16)

    @pl.when(pl.program_id(0) == 0)
    def _():
        h_ref[...] = normed(x_ref[...], mod_ref[...])
        hidden_chunk(0)

    for c in range(1, D_FF // FF_CHUNK):
        hidden_chunk(c)
    y = _dot(act_ref[...], wout_ref[...])
    gate = mod_ref[...][3 * sub + 2:3 * sub + 3]
    o_ref[...] = x_ref[...] + (0.5 * gate) * y
    h_ref[...] = normed(xn_ref[...], modn_ref[...])
    hidden_chunk(0)


def _const_spec(shape):
    return pl.BlockSpec(shape, lambda *_: (0,) * len(shape), pipeline_mode=pl.Buffered(1))


def _ffn(x2d, mod, g, w_in, w_out, *, sub, seq):
    tokens = x2d.shape[0]
    tiles_per_seq = seq // TM_FFN
    n_tiles = tokens // TM_FFN
    nxt = lambda i: jnp.minimum(i + 1, n_tiles - 1)
    return pl.pallas_call(
        functools.partial(_ffn_kernel, sub=sub),
        grid=(n_tiles,),
        in_specs=[
            pl.BlockSpec((TM_FFN, D_MODEL), lambda i: (i, 0)),
            pl.BlockSpec((pl.Squeezed(), N_MOD_ROWS, D_MODEL), lambda i: (i // tiles_per_seq, 0, 0)),
            pl.BlockSpec((TM_FFN, D_MODEL), lambda i: (nxt(i), 0)),
            pl.BlockSpec((pl.Squeezed(), N_MOD_ROWS, D_MODEL), lambda i: (nxt(i) // tiles_per_seq, 0, 0)),
            _const_spec((1, D_MODEL)),
            _const_spec((D_MODEL, 2 * D_FF)),
            _const_spec((D_FF, D_MODEL)),
        ],
        out_specs=pl.BlockSpec((TM_FFN, D_MODEL), lambda i: (i, 0)),
        out_shape=jax.ShapeDtypeStruct((tokens, D_MODEL), F32),
        scratch_shapes=[pltpu.VMEM((TM_FFN, D_MODEL), BF16), pltpu.VMEM((TM_FFN, D_FF), BF16)],
        compiler_params=pltpu.CompilerParams(
            dimension_semantics=("arbitrary",), vmem_limit_bytes=VMEM_LIMIT),
        name=f"ffn{sub}",
    )(x2d, mod, x2d, mod, g, w_in, w_out)


def _segment_mean_sq(z, seg):
    sq = z * z
    hi = sq.astype(BF16)
    lo = (sq - hi.astype(F32)).astype(BF16)
    width = seg.shape[0]
    outs = []
    for p in range(z.shape[1] // width):
        sl = slice(p * width, (p + 1) * width)
        outs.append(_dot(hi[:, sl], seg) + _dot(lo[:, sl], seg))
    return jnp.concatenate(outs, axis=1)


def _lanes_to_column(p):
    width = p.shape[1]
    diag = (lax.broadcasted_iota(jnp.int32, (width, width), 0)
            == lax.broadcasted_iota(jnp.int32, (width, width), 1))
    cols = [jnp.sum(jnp.where(diag, jnp.broadcast_to(p[a:a + 1, :], (width, width)), 0.0),
                    axis=1, keepdims=True) for a in range(p.shape[0])]
    return jnp.concatenate(cols, axis=0)


def _rope_head(z, cos_t, sin_lo, sin_hi):
    up = pltpu.roll(z, HEAD_PAD - QK_ROPE_DIM // 2, 1)
    down = pltpu.roll(z, QK_ROPE_DIM // 2, 1)
    return z * cos_t + up * sin_lo + down * sin_hi


def _mix_in_kernel(x_ref, mod_ref, pos_ref, gmix_ref, win_ref, pgrp_ref, pscale_ref, wpp_ref,
                   qan_ref, wq_ref, kvan_ref, wkv_ref, qg_ref, kg_ref, krg_ref, seg_ref, freq_ref,
                   q_out, kt_out, v_out, p_out, g_out, ext_ref, *, tiles_per_seq):
    tm = x_ref.shape[0]
    tile_in_seq = pl.program_id(0) % tiles_per_seq
    pair_width = 2 * HEAD_PAD
    seg = seg_ref[...]
    x = x_ref[...]
    mod = mod_ref[...]
    h = _norm_mod(x, gmix_ref[...], mod[3:4], mod[4:5]).astype(BF16)

    qk = _dot(h, win_ref[:, OFF_QK:OFF_QK + Q_LORA_RANK + HEAD_PAD])
    q_lat, k_rope = qk[:, :Q_LORA_RANK], qk[:, Q_LORA_RANK:]
    kv_lat = _dot(h, win_ref[:, OFF_KV:OFF_KV + KV_LORA_RANK])
    u = _dot(h, win_ref[:, OFF_U:OFF_U + POOL_WIDTH])

    lane = lax.broadcasted_iota(jnp.int32, (1, HEAD_PAD), 1)
    ang = _lanes_to_column(pos_ref[...].astype(F32)) * freq_ref[...]
    cos_a, sin_a = jnp.cos(ang), jnp.sin(ang)
    half = QK_ROPE_DIM // 2
    cos_t = jnp.where(lane < QK_NOPE_DIM, 1.0, cos_a)
    sin_lo = jnp.where((lane >= QK_NOPE_DIM) & (lane < QK_NOPE_DIM + half), -sin_a, 0.0)
    sin_hi = jnp.where((lane >= QK_NOPE_DIM + half) & (lane < QK_NOPE_DIM + 2 * half), sin_a, 0.0)

    q = _dot(_rms_norm(q_lat, qan_ref[...]).astype(BF16), wq_ref[...])
    kvn = _rms_norm(kv_lat, kvan_ref[...]).astype(BF16)
    k_nope = _dot(kvn, wkv_ref[:, 0:N_HEADS * HEAD_PAD])
    v_out[...] = _dot(kvn, wkv_ref[:, N_HEADS * HEAD_PAD:]).astype(BF16)

    @pl.when(tile_in_seq == 0)
    def _():
        ext_ref[0:POOL_HALO, :] = jnp.zeros((POOL_HALO, POOL_WIDTH), F32)

    @pl.when(tile_in_seq != 0)
    def _():
        ext_ref[0:POOL_HALO, :] = ext_ref[tm:tm + POOL_HALO, :]

    ext_ref[POOL_HALO:POOL_HALO + tm, :] = u
    t_in_seq = tile_in_seq * tm + lax.broadcasted_iota(jnp.int32, (tm, 1), 0)
    pooled = []
    for grp, window in enumerate(POOL_WINDOWS):
        lanes = slice(grp * POOL_GROUP_DIM, (grp + 1) * POOL_GROUP_DIM)
        u_g = u[:, lanes]
        acc = u_g
        for back in range(1, window):
            acc = acc + ext_ref[POOL_HALO - back:POOL_HALO - back + tm, lanes]
        cnt = jnp.minimum(t_in_seq + 1, window).astype(F32)
        pooled.append(_dot((acc / cnt - u_g).astype(BF16), pgrp_ref[grp]))
    pooled = jnp.concatenate(pooled, axis=1) * pscale_ref[...]
    br_pool = _dot(pooled.astype(BF16), wpp_ref[...])

    kr_ms = jnp.sum(k_rope * k_rope, axis=-1, keepdims=True) * (1.0 / QK_ROPE_DIM)
    kr = _rope_head(k_rope * lax.rsqrt(kr_ms + NORM_EPS) * krg_ref[...], cos_t, sin_lo, sin_hi)

    assert D_MODEL == N_HEADS * HEAD_PAD
    for pair in range(N_HEADS // 2):
        lanes = slice(pair * pair_width, (pair + 1) * pair_width)
        g_pool = _dot(h, win_ref[:, OFF_GP + pair * pair_width:OFF_GP + (pair + 1) * pair_width])
        p_out[:, lanes] = jax.nn.sigmoid(g_pool) * br_pool[:, lanes]
        q_p = q[:, lanes]
        qn = q_p * lax.rsqrt(_segment_mean_sq(q_p, seg) + NORM_EPS) * qg_ref[:, lanes]
        for sub in range(2):
            hd = 2 * pair + sub
            roped = _rope_head(qn[:, sub * HEAD_PAD:(sub + 1) * HEAD_PAD], cos_t, sin_lo, sin_hi)
            q_out[:, hd * HEAD_PAD:(hd + 1) * HEAD_PAD] = (roped * Q_SCALE).astype(BF16)
        g_mla = _dot(h, win_ref[:, OFF_GM + pair * pair_width:OFF_GM + (pair + 1) * pair_width])
        g_out[:, lanes] = jax.nn.sigmoid(g_mla)
        k_p = k_nope[:, lanes]
        kn = k_p * lax.rsqrt(_segment_mean_sq(k_p, seg) + NORM_EPS) * kg_ref[:, lanes]
        for sub in range(2):
            hd = 2 * pair + sub
            k_h = kn[:, sub * HEAD_PAD:(sub + 1) * HEAD_PAD] + kr
            kt_out[hd * HEAD_PAD:(hd + 1) * HEAD_PAD, :] = k_h.T.astype(BF16)


def _mix_in(x2d, mod, pos2d, gmix, w, *, seq):
    tokens = x2d.shape[0]
    tm = TM_MIX
    tiles_per_seq = seq // tm
    tok_spec = lambda width: pl.BlockSpec((tm, width), lambda i: (i, 0))
    consts = [gmix, w["w_in"], w["pool_grp"], w["pool_scale"], w["w_pool_proj"], w["q_a_norm"], w["w_q"],
              w["kv_a_norm"], w["w_kv"], w["q_gain"], w["k_gain"], w["kr_gain"], w["seg"], w["freq"]]
    return pl.pallas_call(
        functools.partial(_mix_in_kernel, tiles_per_seq=tiles_per_seq),
        grid=(tokens // tm,),
        in_specs=[
            tok_spec(D_MODEL),
            pl.BlockSpec((pl.Squeezed(), N_MOD_ROWS, D_MODEL), lambda i: (i // tiles_per_seq, 0, 0)),
            pl.BlockSpec((pl.Squeezed(), tm // HEAD_PAD, HEAD_PAD), lambda i: (i, 0, 0)),
        ] + [_const_spec(a.shape) for a in consts],
        out_specs=[tok_spec(N_HEADS * HEAD_PAD),
                   pl.BlockSpec((pl.Squeezed(), N_HEADS * HEAD_PAD, tm),
                                lambda i: (i // tiles_per_seq, 0, i % tiles_per_seq)),
                   tok_spec(MLA_WIDTH), tok_spec(D_MODEL), tok_spec(D_MODEL)],
        out_shape=[
            jax.ShapeDtypeStruct((tokens, N_HEADS * HEAD_PAD), BF16),
            jax.ShapeDtypeStruct((tokens // seq, N_HEADS * HEAD_PAD, seq), BF16),
            jax.ShapeDtypeStruct((tokens, MLA_WIDTH), BF16),
            jax.ShapeDtypeStruct((tokens, D_MODEL), F32),
            jax.ShapeDtypeStruct((tokens, D_MODEL), F32),
        ],
        scratch_shapes=[pltpu.VMEM((POOL_HALO + tm, POOL_WIDTH), F32)],
        compiler_params=pltpu.CompilerParams(
            dimension_semantics=("arbitrary",), vmem_limit_bytes=VMEM_LIMIT),
        name="mix_in",
    )(x2d, mod, pos2d.reshape(tokens // tm, tm // HEAD_PAD, HEAD_PAD), *consts)


def _attn_kernel(q_ref, kt_ref, v_ref, p_ref, g_ref, x_ref, mod_ref, wmla_ref, wout_ref, o_ref,
                 s_ref, attn_ref):
    j = pl.program_id(1)
    seq = kt_ref.shape[1]
    row = lax.broadcasted_iota(jnp.int32, (TK, TK), 0)
    col = lax.broadcasted_iota(jnp.int32, (TK, TK), 1)
    causal = col <= row
    lane = lax.broadcasted_iota(jnp.int32, (1, 2 * V_HEAD_DIM), 1)

    def lane_fold(s, op):
        out = s[:, 0:128]
        for t in range(1, TK // 128):
            out = op(out, s[:, t * 128:(t + 1) * 128])
        return out

    def score_pass(unit, hd, block, klen):
        slot = unit % S_SLOTS
        lanes = slice(hd * HEAD_PAD, (hd + 1) * HEAD_PAD)
        q_h = q_ref[block * TK:(block + 1) * TK, lanes]
        n_chunks = klen // TK
        m_run = None
        for c in range(n_chunks):
            cols = slice(c * TK, (c + 1) * TK)
            s = _dot(q_h, kt_ref[lanes, cols])
            if c == n_chunks - 1:
                s = jnp.where(causal, s, MASK_VALUE)
            s_ref[slot, :, cols] = s
            fold = lane_fold(s, jnp.maximum)
            m_run = fold if m_run is None else jnp.maximum(m_run, fold)
        return jnp.max(m_run, axis=-1, keepdims=True)

    def value_pass(unit, hd, klen, m):
        slot = unit % S_SLOTS
        v_lanes = slice((hd // 2) * 128, (hd // 2 + 1) * 128)
        l_run = acc = None
        for c in range(klen // TK):
            cols = slice(c * TK, (c + 1) * TK)
            e = jnp.exp2(s_ref[slot, :, cols] - m)
            fold = lane_fold(e, jnp.add)
            l_run = fold if l_run is None else l_run + fold
            pv = _dot(e.astype(BF16), v_ref[cols, v_lanes])
            acc = pv if acc is None else acc + pv
        return acc / jnp.sum(l_run, axis=-1, keepdims=True)

    def all_units(first_key_end):
        units = [(2 * pair + sub, block) for pair in range(N_HEADS // 2) for block in range(TQ // TK)
                 for sub in range(2)]
        klen = lambda block: first_key_end + block * TK
        row_max, out = {}, {}
        for n in range(len(units) + S_SLOTS):
            done = n - S_SLOTS
            if done >= 0:
                hd, block = units[done]
                out[done] = value_pass(done, hd, klen(block), row_max.pop(done))
                if done % 2 == 1:
                    both = jnp.where(lane < V_HEAD_DIM, out.pop(done - 1), out.pop(done))
                    attn_ref[block * TK:(block + 1) * TK, (hd // 2) * 128:(hd // 2 + 1) * 128] = both.astype(BF16)
            if n < len(units):
                hd, block = units[n]
                row_max[n] = score_pass(n, hd, block, klen(block))

    for variant in range(seq // TQ):
        @pl.when(j == variant)
        def _(variant=variant):
            all_units(variant * TQ + TK)

    br_mla = _dot(attn_ref[...], wmla_ref[...])
    merged = p_ref[...] + g_ref[...] * br_mla
    gate = mod_ref[...][5:6]
    o_ref[...] = x_ref[...] + gate * _dot(merged.astype(BF16), wout_ref[...])


def _attn(q, kt, v, p, g, x2d, mod, w_mla, w_out, *, batch, seq):
    tokens = x2d.shape[0]
    nq = seq // TQ
    tok_spec = lambda width: pl.BlockSpec((TQ, width), lambda b, j: (b * nq + j, 0))
    return pl.pallas_call(
        _attn_kernel,
        grid=(batch, nq),
        in_specs=[
            tok_spec(N_HEADS * HEAD_PAD),
            pl.BlockSpec((pl.Squeezed(), N_HEADS * HEAD_PAD, seq), lambda b, j: (b, 0, 0)),
            pl.BlockSpec((seq, MLA_WIDTH), lambda b, j: (b, 0)),
            tok_spec(D_MODEL), tok_spec(D_MODEL), tok_spec(D_MODEL),
            pl.BlockSpec((pl.Squeezed(), N_MOD_ROWS, D_MODEL), lambda b, j: (b, 0, 0)),
            _const_spec((MLA_WIDTH, D_MODEL)), _const_spec((D_MODEL, D_MODEL)),
        ],
        out_specs=tok_spec(D_MODEL),
        out_shape=jax.ShapeDtypeStruct((tokens, D_MODEL), F32),
        scratch_shapes=[pltpu.VMEM((S_SLOTS, TK, seq), F32), pltpu.VMEM((TQ, MLA_WIDTH), BF16)],
        compiler_params=pltpu.CompilerParams(
            dimension_semantics=("arbitrary", "arbitrary"), vmem_limit_bytes=VMEM_LIMIT),
        name="attn",
    )(q, kt, v, p, g, x2d, mod, w_mla, w_out)


def _prep_mixer_weights(w_in, pool_grp, pool_scale, w_pool_proj, q_a_norm, w_q_up, kv_a_norm, w_kv_up,
                        q_norm_nope, q_norm_rope, k_norm_nope, k_norm_rope):
    splits = np.cumsum([POOL_WIDTH, Q_LORA_RANK, KV_LORA_RANK, QK_ROPE_DIM, D_MODEL])
    u_w, q_w, kv_w, kr_w, gp_w, gm_w = jnp.split(w_in, splits, axis=1)
    kr_w = jnp.pad(kr_w, ((0, 0), (QK_NOPE_DIM, HEAD_PAD - QK_NOPE_DIM - QK_ROPE_DIM)))
    w_in_p = jnp.concatenate([u_w, q_w, kr_w, kv_w, gp_w, gm_w], axis=1).astype(BF16)

    qk_head = QK_NOPE_DIM + QK_ROPE_DIM
    w_q = jnp.pad(w_q_up.reshape(Q_LORA_RANK, N_HEADS, qk_head), ((0, 0), (0, 0), (0, HEAD_PAD - qk_head)))
    w_q = w_q.reshape(Q_LORA_RANK, N_HEADS * HEAD_PAD).astype(BF16)
    kv = w_kv_up.reshape(KV_LORA_RANK, N_HEADS, QK_NOPE_DIM + V_HEAD_DIM)
    w_k = jnp.pad(kv[..., :QK_NOPE_DIM], ((0, 0), (0, 0), (0, HEAD_PAD - QK_NOPE_DIM)))
    w_kv = jnp.concatenate([w_k.reshape(KV_LORA_RANK, N_HEADS * HEAD_PAD),
                            kv[..., QK_NOPE_DIM:].reshape(KV_LORA_RANK, MLA_WIDTH)], axis=1).astype(BF16)

    zeros = lambda n: jnp.zeros((n,), F32)
    q_gain = jnp.tile(jnp.concatenate([q_norm_nope, q_norm_rope, zeros(HEAD_PAD - qk_head)]), N_HEADS)
    k_gain = jnp.tile(jnp.concatenate([k_norm_nope, zeros(HEAD_PAD - QK_NOPE_DIM)]), N_HEADS)
    kr_gain = jnp.concatenate([zeros(QK_NOPE_DIM), k_norm_rope, zeros(HEAD_PAD - qk_head)])

    seg = np.zeros((2 * HEAD_PAD, 2 * HEAD_PAD), np.float32)
    for base in (0, HEAD_PAD):
        seg[base:base + QK_NOPE_DIM, base:base + QK_NOPE_DIM] = 1.0 / QK_NOPE_DIM
        seg[base + QK_NOPE_DIM:base + qk_head, base + QK_NOPE_DIM:base + qk_head] = 1.0 / QK_ROPE_DIM

    inv_freq = ROPE_THETA ** (-jnp.arange(0, QK_ROPE_DIM, 2, dtype=F32) / QK_ROPE_DIM)
    freq = jnp.concatenate([zeros(QK_NOPE_DIM), inv_freq, inv_freq, zeros(HEAD_PAD - qk_head)])

    return {
        "w_in": w_in_p, "pool_grp": pool_grp.astype(BF16), "pool_scale": pool_scale.reshape(1, POOL_WIDTH),
        "w_pool_proj": w_pool_proj.astype(BF16), "q_a_norm": q_a_norm.reshape(1, Q_LORA_RANK), "w_q": w_q,
        "kv_a_norm": kv_a_norm.reshape(1, KV_LORA_RANK), "w_kv": w_kv,
        "q_gain": q_gain.reshape(1, -1), "k_gain": k_gain.reshape(1, -1), "kr_gain": kr_gain.reshape(1, -1),
        "seg": jnp.asarray(seg, BF16), "freq": freq.reshape(1, HEAD_PAD),
    }


def kernel(x, c, positions, w_ada, b_ada, norm_ffn1, w_ffn1_in, w_ffn1_out, norm_mix, w_in, pool_grp,
           pool_scale, w_pool_proj, q_a_norm, w_q_up, kv_a_norm, w_kv_up, q_norm_nope, q_norm_rope,
           k_norm_nope, k_norm_rope, w_mla_proj, w_out, norm_ffn2, w_ffn2_in, w_ffn2_out):
    batch, seq, d = x.shape
    depth = w_ada.shape[0]
    assert d == D_MODEL and seq % TM_FFN == 0 and seq % TM_MIX == 0
    assert seq % TQ == 0 and TQ % TK == 0
    x2d = x.reshape(batch * seq, d)
    pos2d = positions.reshape(batch * seq // HEAD_PAD, HEAD_PAD)
    for l in range(depth):
        mod = _ada(c, w_ada[l], b_ada[l]).reshape(batch, N_MOD_ROWS, d)
        x2d = _ffn(x2d, mod, norm_ffn1[l].reshape(1, d), w_ffn1_in[l].astype(BF16),
                   w_ffn1_out[l].astype(BF16), sub=0, seq=seq)
        w = _prep_mixer_weights(w_in[l], pool_grp[l], pool_scale[l], w_pool_proj[l], q_a_norm[l], w_q_up[l],
                                kv_a_norm[l], w_kv_up[l], q_norm_nope[l], q_norm_rope[l], k_norm_nope[l],
                                k_norm_rope[l])
        q, kt, v, p, g = _mix_in(x2d, mod, pos2d, norm_mix[l].reshape(1, d), w, seq=seq)
        x2d = _attn(q, kt, v, p, g, x2d, mod, w_mla_proj[l].astype(BF16), w_out[l].astype(BF16),
                    batch=batch, seq=seq)
        x2d = _ffn(x2d, mod, norm_ffn2[l].reshape(1, d), w_ffn2_in[l].astype(BF16),
                   w_ffn2_out[l].astype(BF16), sub=2, seq=seq)
    return x2d.reshape(batch, seq, d)
```

```python
import functools
import math

import numpy as np
import jax
import jax.numpy as jnp
from jax import lax
from jax.experimental import pallas as pl
from jax.experimental.pallas import tpu as pltpu

F32 = jnp.float32
BF16 = jnp.bfloat16

D_MODEL = 1024
D_FF = 2816
N_MOD_ROWS = 9
POOL_WINDOWS = (2, 4, 8, 16)
POOL_WIDTH = 512
POOL_GROUP_DIM = 128
POOL_HALO = 16
N_HEADS = 8
QK_NOPE_DIM = 64
QK_ROPE_DIM = 32
V_HEAD_DIM = 64
HEAD_PAD = 128
Q_LORA_RANK = 384
KV_LORA_RANK = 256
MLA_WIDTH = N_HEADS * V_HEAD_DIM
ROPE_THETA = 10000.0
ATTN_SCALE = 1.0 / math.sqrt(QK_NOPE_DIM + QK_ROPE_DIM)
Q_SCALE = ATTN_SCALE * math.log2(math.e)
NORM_EPS = 1e-6
MASK_VALUE = -1e30

OFF_U = 0
OFF_QK = POOL_WIDTH
OFF_KV = OFF_QK + Q_LORA_RANK + HEAD_PAD
OFF_GP = OFF_KV + KV_LORA_RANK
OFF_GM = OFF_GP + D_MODEL
IN_WIDTH_PAD = OFF_GM + D_MODEL

TM_FFN = 512
TM_MIX = 512
TQ = 256
TK = 256
S_SLOTS = 4
FF_CHUNK = 256
EARLY_GATE_CHUNKS = 2
ADA_BLOCK = 1152
VMEM_LIMIT = 56 * 1024 * 1024


def _dot(a, b):
    return jnp.dot(a, b, preferred_element_type=F32)


def _rms_norm(x, g):
    return x * lax.rsqrt(jnp.mean(x * x, axis=-1, keepdims=True) + NORM_EPS) * g


def _norm_mod(x, g, shift, scale):
    return _rms_norm(x, g) * (1.0 + scale) + shift


def _ada_kernel(c_ref, w_ref, b_ref, o_ref):
    c = c_ref[...]
    c_act = (c * jax.nn.sigmoid(c)).astype(BF16)
    o_ref[...] = _dot(c_act, w_ref[...].astype(BF16)) + b_ref[...]


def _ada(c, w_ada, b_ada):
    batch = c.shape[0]
    n = w_ada.shape[1]
    return pl.pallas_call(
        _ada_kernel,
        grid=(n // ADA_BLOCK,),
        in_specs=[
            pl.BlockSpec((batch, D_MODEL), lambda i: (0, 0)),
            pl.BlockSpec((D_MODEL, ADA_BLOCK), lambda i: (0, i)),
            pl.BlockSpec((1, ADA_BLOCK), lambda i: (0, i)),
        ],
        out_specs=pl.BlockSpec((batch, ADA_BLOCK), lambda i: (0, i)),
        out_shape=jax.ShapeDtypeStruct((batch, n), F32),
        compiler_params=pltpu.CompilerParams(
            dimension_semantics=("arbitrary",), vmem_limit_bytes=VMEM_LIMIT),
        name="ada",
    )(c, w_ada, b_ada.reshape(1, n))


def _ffn_kernel(x_ref, mod_ref, xn_ref, modn_ref, g_ref, win_ref, wout_ref, o_ref, h_ref, act_ref, *, sub):
    def normed(x, mod):
        return _norm_mod(x, g_ref[...], mod[3 * sub:3 * sub + 1], mod[3 * sub + 1:3 * sub + 2]).astype(BF16)

    def hidden_chunk(c):
        lo = c * FF_CHUNK
        g = _dot(h_ref[...], win_ref[:, lo:lo + FF_CHUNK])
        u = _dot(h_ref[...], win_ref[:, D_FF + lo:D_FF + lo + FF_CHUNK])
        act_ref[:, lo:lo + FF_CHUNK] = (g * jax.nn.sigmoid(g) * u).astype(BF16)

    @pl.when(pl.program_id(0) == 0)
    def _():
        h_ref[...] = normed(x_ref[...], mod_ref[...])
        hidden_chunk(0)

    for c in range(1, D_FF // FF_CHUNK):
        hidden_chunk(c)
    y = _dot(act_ref[...], wout_ref[...])
    gate = mod_ref[...][3 * sub + 2:3 * sub + 3]
    o_ref[...] = x_ref[...] + (0.5 * gate) * y
    h_ref[...] = normed(xn_ref[...], modn_ref[...])
    hidden_chunk(0)


def _const_spec(shape):
    return pl.BlockSpec(shape, lambda *_: (0,) * len(shape), pipeline_mode=pl.Buffered(1))


def _ffn(x2d, mod, g, w_in, w_out, *, sub, seq):
    tokens = x2d.shape[0]
    tiles_per_seq = seq // TM_FFN
    n_tiles = tokens // TM_FFN
    nxt = lambda i: jnp.minimum(i + 1, n_tiles - 1)
    return pl.pallas_call(
        functools.partial(_ffn_kernel, sub=sub),
        grid=(n_tiles,),
        in_specs=[
            pl.BlockSpec((TM_FFN, D_MODEL), lambda i: (i, 0)),
            pl.BlockSpec((pl.Squeezed(), N_MOD_ROWS, D_MODEL), lambda i: (i // tiles_per_seq, 0, 0)),
            pl.BlockSpec((TM_FFN, D_MODEL), lambda i: (nxt(i), 0)),
            pl.BlockSpec((pl.Squeezed(), N_MOD_ROWS, D_MODEL), lambda i: (nxt(i) // tiles_per_seq, 0, 0)),
            _const_spec((1, D_MODEL)),
            _const_spec((D_MODEL, 2 * D_FF)),
            _const_spec((D_FF, D_MODEL)),
        ],
        out_specs=pl.BlockSpec((TM_FFN, D_MODEL), lambda i: (i, 0)),
        out_shape=jax.ShapeDtypeStruct((tokens, D_MODEL), F32),
        scratch_shapes=[pltpu.VMEM((TM_FFN, D_MODEL), BF16), pltpu.VMEM((TM_FFN, D_FF), BF16)],
        compiler_params=pltpu.CompilerParams(
            dimension_semantics=("arbitrary",), vmem_limit_bytes=VMEM_LIMIT),
        name=f"ffn{sub}",
    )(x2d, mod, x2d, mod, g, w_in, w_out)


def _segment_mean_sq(z, seg):
    sq = z * z
    hi = sq.astype(BF16)
    lo = (sq - hi.astype(F32)).astype(BF16)
    width = seg.shape[0]
    outs = []
    for p in range(z.shape[1] // width):
        sl = slice(p * width, (p + 1) * width)
        outs.append(_dot(hi[:, sl], seg) + _dot(lo[:, sl], seg))
    return jnp.concatenate(outs, axis=1)


def _lanes_to_column(p):
    width = p.shape[1]
    diag = (lax.broadcasted_iota(jnp.int32, (width, width), 0)
            == lax.broadcasted_iota(jnp.int32, (width, width), 1))
    cols = [jnp.sum(jnp.where(diag, jnp.broadcast_to(p[a:a + 1, :], (width, width)), 0.0),
                    axis=1, keepdims=True) for a in range(p.shape[0])]
    return jnp.concatenate(cols, axis=0)


def _rope_head(z, cos_t, sin_lo, sin_hi):
    up = pltpu.roll(z, HEAD_PAD - QK_ROPE_DIM // 2, 1)
    down = pltpu.roll(z, QK_ROPE_DIM // 2, 1)
    return z * cos_t + up * sin_lo + down * sin_hi


def _mix_in_kernel(x_ref, mod_ref, pos_ref, gmix_ref, win_ref, pgrp_ref, pscale_ref, wpp_ref,
                   qan_ref, wq_ref, kvan_ref, wkv_ref, qg_ref, kg_ref, krg_ref, seg_ref, freq_ref,
                   q_out, kt_out, v_out, p_out, g_out, ext_ref, *, tiles_per_seq):
    tm = x_ref.shape[0]
    tile_in_seq = pl.program_id(0) % tiles_per_seq
    pair_width = 2 * HEAD_PAD

    @pl.when(pl.program_id(0) == 0)
    def _():
        ext_ref[...] = jnp.zeros(ext_ref.shape, F32)

    seg = seg_ref[...]
    x = x_ref[...]
    mod = mod_ref[...]
    h = _norm_mod(x, gmix_ref[...], mod[3:4], mod[4:5]).astype(BF16)

    qk = _dot(h, win_ref[:, OFF_QK:OFF_QK + Q_LORA_RANK + HEAD_PAD])
    q_lat, k_rope = qk[:, :Q_LORA_RANK], qk[:, Q_LORA_RANK:]
    kv_lat = _dot(h, win_ref[:, OFF_KV:OFF_KV + KV_LORA_RANK])
    u = _dot(h, win_ref[:, OFF_U:OFF_U + POOL_WIDTH])

    lane = lax.broadcasted_iota(jnp.int32, (1, HEAD_PAD), 1)
    ang = _lanes_to_column(pos_ref[...].astype(F32)) * freq_ref[...]
    cos_a, sin_a = jnp.cos(ang), jnp.sin(ang)
    half = QK_ROPE_DIM // 2
    cos_t = jnp.where(lane < QK_NOPE_DIM, 1.0, cos_a)
    sin_lo = jnp.where((lane >= QK_NOPE_DIM) & (lane < QK_NOPE_DIM + half), -sin_a, 0.0)
    sin_hi = jnp.where((lane >= QK_NOPE_DIM + half) & (lane < QK_NOPE_DIM + 2 * half), sin_a, 0.0)

    q = _dot(_rms_norm(q_lat, qan_ref[...]).astype(BF16), wq_ref[...])
    kvn = _rms_norm(kv_lat, kvan_ref[...]).astype(BF16)
    k_nope = _dot(kvn, wkv_ref[:, 0:N_HEADS * HEAD_PAD])
    v_out[...] = _dot(kvn, wkv_ref[:, N_HEADS * HEAD_PAD:]).astype(BF16)

    def mla_gate_chunk(pair):
        lanes = slice(pair * pair_width, (pair + 1) * pair_width)
        g_mla = _dot(h, win_ref[:, OFF_GM + pair * pair_width:OFF_GM + (pair + 1) * pair_width])
        g_out[:, lanes] = jax.nn.sigmoid(g_mla)

    for pair in range(EARLY_GATE_CHUNKS):
        mla_gate_chunk(pair)

    ext_ref[0:POOL_HALO, :] = jnp.where(tile_in_seq == 0, 0.0, ext_ref[tm:tm + POOL_HALO, :])
    ext_ref[POOL_HALO:POOL_HALO + tm, :] = u
    t_in_seq = tile_in_seq * tm + lax.broadcasted_iota(jnp.int32, (tm, 1), 0)
    pooled = []
    for grp, window in enumerate(POOL_WINDOWS):
        lanes = slice(grp * POOL_GROUP_DIM, (grp + 1) * POOL_GROUP_DIM)
        u_g = u[:, lanes]
        acc = u_g
        for back in range(1, window):
            acc = acc + ext_ref[POOL_HALO - back:POOL_HALO - back + tm, lanes]
        cnt = jnp.minimum(t_in_seq + 1, window).astype(F32)
        pooled.append(_dot((acc / cnt - u_g).astype(BF16), pgrp_ref[grp]))
    pooled = jnp.concatenate(pooled, axis=1) * pscale_ref[...]
    br_pool = _dot(pooled.astype(BF16), wpp_ref[...])

    kr_ms = jnp.sum(k_rope * k_rope, axis=-1, keepdims=True) * (1.0 / QK_ROPE_DIM)
    kr = _rope_head(k_rope * lax.rsqrt(kr_ms + NORM_EPS) * krg_ref[...], cos_t, sin_lo, sin_hi)

    assert D_MODEL == N_HEADS * HEAD_PAD
    for pair in range(N_HEADS // 2):
        lanes = slice(pair * pair_width, (pair + 1) * pair_width)
        g_pool = _dot(h, win_ref[:, OFF_GP + pair * pair_width:OFF_GP + (pair + 1) * pair_width])
        p_out[:, lanes] = jax.nn.sigmoid(g_pool) * br_pool[:, lanes]
        q_p = q[:, lanes]
        qn = q_p * lax.rsqrt(_segment_mean_sq(q_p, seg) + NORM_EPS) * qg_ref[:, lanes]
        for sub in range(2):
            hd = 2 * pair + sub
            roped = _rope_head(qn[:, sub * HEAD_PAD:(sub + 1) * HEAD_PAD], cos_t, sin_lo, sin_hi)
            q_out[:, hd * HEAD_PAD:(hd + 1) * HEAD_PAD] = (roped * Q_SCALE).astype(BF16)
        if pair + EARLY_GATE_CHUNKS < N_HEADS // 2:
            mla_gate_chunk(pair + EARLY_GATE_CHUNKS)
        k_p = k_nope[:, lanes]
        kn = k_p * lax.rsqrt(_segment_mean_sq(k_p, seg) + NORM_EPS) * kg_ref[:, lanes]
        for sub in range(2):
            hd = 2 * pair + sub
            k_h = kn[:, sub * HEAD_PAD:(sub + 1) * HEAD_PAD] + kr
            kt_out[hd * HEAD_PAD:(hd + 1) * HEAD_PAD, :] = k_h.T.astype(BF16)


def _mix_in(x2d, mod, pos2d, gmix, w, *, seq):
    tokens = x2d.shape[0]
    tm = TM_MIX
    tiles_per_seq = seq // tm
    tok_spec = lambda width: pl.BlockSpec((tm, width), lambda i: (i, 0))
    consts = [gmix, w["w_in"], w["pool_grp"], w["pool_scale"], w["w_pool_proj"], w["q_a_norm"], w["w_q"],
              w["kv_a_norm"], w["w_kv"], w["q_gain"], w["k_gain"], w["kr_gain"], w["seg"], w["freq"]]
    return pl.pallas_call(
        functools.partial(_mix_in_kernel, tiles_per_seq=tiles_per_seq),
        grid=(tokens // tm,),
        in_specs=[
            tok_spec(D_MODEL),
            pl.BlockSpec((pl.Squeezed(), N_MOD_ROWS, D_MODEL), lambda i: (i // tiles_per_seq, 0, 0)),
            pl.BlockSpec((pl.Squeezed(), tm // HEAD_PAD, HEAD_PAD), lambda i: (i, 0, 0)),
        ] + [_const_spec(a.shape) for a in consts],
        out_specs=[tok_spec(N_HEADS * HEAD_PAD),
                   pl.BlockSpec((pl.Squeezed(), N_HEADS * HEAD_PAD, tm),
                                lambda i: (i // tiles_per_seq, 0, i % tiles_per_seq)),
                   tok_spec(MLA_WIDTH), tok_spec(D_MODEL), tok_spec(D_MODEL)],
        out_shape=[
            jax.ShapeDtypeStruct((tokens, N_HEADS * HEAD_PAD), BF16),
            jax.ShapeDtypeStruct((tokens // seq, N_HEADS * HEAD_PAD, seq), BF16),
            jax.ShapeDtypeStruct((tokens, MLA_WIDTH), BF16),
            jax.ShapeDtypeStruct((tokens, D_MODEL), F32),
            jax.ShapeDtypeStruct((tokens, D_MODEL), F32),
        ],
        scratch_shapes=[pltpu.VMEM((POOL_HALO + tm, POOL_WIDTH), F32)],
        compiler_params=pltpu.CompilerParams(
            dimension_semantics=("arbitrary",), vmem_limit_bytes=VMEM_LIMIT),
        name="mix_in",
    )(x2d, mod, pos2d.reshape(tokens // tm, tm // HEAD_PAD, HEAD_PAD), *consts)


def _attn_kernel(q_ref, kt_ref, v_ref, p_ref, g_ref, x_ref, mod_ref, wmla_ref, wout_ref, o_ref,
                 s_ref, attn_ref):
    j = pl.program_id(1)
    seq = kt_ref.shape[1]
    row = lax.broadcasted_iota(jnp.int32, (TK, TK), 0)
    col = lax.broadcasted_iota(jnp.int32, (TK, TK), 1)
    causal = col <= row
    lane = lax.broadcasted_iota(jnp.int32, (1, 2 * V_HEAD_DIM), 1)

    def lane_fold(s, op):
        out = s[:, 0:128]
        for t in range(1, TK // 128):
            out = op(out, s[:, t * 128:(t + 1) * 128])
        return out

    def score_pass(unit, hd, block, klen):
        slot = unit % S_SLOTS
        lanes = slice(hd * HEAD_PAD, (hd + 1) * HEAD_PAD)
        q_h = q_ref[block * TK:(block + 1) * TK, lanes]
        n_chunks = klen // TK
        m_run = None
        for c in range(n_chunks):
            cols = slice(c * TK, (c + 1) * TK)
            s = _dot(q_h, kt_ref[lanes, cols])
            if c == n_chunks - 1:
                s = jnp.where(causal, s, MASK_VALUE)
            s_ref[slot, :, cols] = s
            fold = lane_fold(s, jnp.maximum)
            m_run = fold if m_run is None else jnp.maximum(m_run, fold)
        return jnp.max(m_run, axis=-1, keepdims=True)

    def value_pass(unit, hd, klen, m):
        slot = unit % S_SLOTS
        v_lanes = slice((hd // 2) * 128, (hd // 2 + 1) * 128)
        l_run = acc = None
        for c in range(klen // TK):
            cols = slice(c * TK, (c + 1) * TK)
            e = jnp.exp2(s_ref[slot, :, cols] - m)
            fold = lane_fold(e, jnp.add)
            l_run = fold if l_run is None else l_run + fold
            pv = _dot(e.astype(BF16), v_ref[cols, v_lanes])
            acc = pv if acc is None else acc + pv
        return acc / jnp.sum(l_run, axis=-1, keepdims=True)

    def all_units(first_key_end):
        units = [(2 * pair + sub, block) for pair in range(N_HEADS // 2) for block in range(TQ // TK)
                 for sub in range(2)]
        klen = lambda block: first_key_end + block * TK
        row_max, out = {}, {}
        for n in range(len(units) + S_SLOTS):
            done = n - S_SLOTS
            if done >= 0:
                hd, block = units[done]
                out[done] = value_pass(done, hd, klen(block), row_max.pop(done))
                if done % 2 == 1:
                    both = jnp.where(lane < V_HEAD_DIM, out.pop(done - 1), out.pop(done))
                    attn_ref[block * TK:(block + 1) * TK, (hd // 2) * 128:(hd // 2 + 1) * 128] = both.astype(BF16)
            if n < len(units):
                hd, block = units[n]
                row_max[n] = score_pass(n, hd, block, klen(block))

    def project():
        br_mla = _dot(attn_ref[...], wmla_ref[...])
        merged = p_ref[...] + g_ref[...] * br_mla
        gate = mod_ref[...][5:6]
        o_ref[...] = x_ref[...] + gate * _dot(merged.astype(BF16), wout_ref[...])

    for variant in range(seq // TQ):
        @pl.when(j == variant)
        def _(variant=variant):
            all_units(variant * TQ + TK)
            project()


def _attn(q, kt, v, p, g, x2d, mod, w_mla, w_out, *, batch, seq):
    tokens = x2d.shape[0]
    nq = seq // TQ
    tok_spec = lambda width: pl.BlockSpec((TQ, width), lambda b, j: (b * nq + j, 0))
    return pl.pallas_call(
        _attn_kernel,
        grid=(batch, nq),
        in_specs=[
            tok_spec(N_HEADS * HEAD_PAD),
            pl.BlockSpec((pl.Squeezed(), N_HEADS * HEAD_PAD, seq), lambda b, j: (b, 0, 0)),
            pl.BlockSpec((seq, MLA_WIDTH), lambda b, j: (b, 0)),
            tok_spec(D_MODEL), tok_spec(D_MODEL), tok_spec(D_MODEL),
            pl.BlockSpec((pl.Squeezed(), N_MOD_ROWS, D_MODEL), lambda b, j: (b, 0, 0)),
            _const_spec((MLA_WIDTH, D_MODEL)), _const_spec((D_MODEL, D_MODEL)),
        ],
        out_specs=tok_spec(D_MODEL),
        out_shape=jax.ShapeDtypeStruct((tokens, D_MODEL), F32),
        scratch_shapes=[pltpu.VMEM((S_SLOTS, TK, seq), F32), pltpu.VMEM((TQ, MLA_WIDTH), BF16)],
        compiler_params=pltpu.CompilerParams(
            dimension_semantics=("arbitrary", "arbitrary"), vmem_limit_bytes=VMEM_LIMIT),
        name="attn",
    )(q, kt, v, p, g, x2d, mod, w_mla, w_out)


def _prep_mixer_weights(w_in, pool_grp, pool_scale, w_pool_proj, q_a_norm, w_q_up, kv_a_norm, w_kv_up,
                        q_norm_nope, q_norm_rope, k_norm_nope, k_norm_rope):
    splits = np.cumsum([POOL_WIDTH, Q_LORA_RANK, KV_LORA_RANK, QK_ROPE_DIM, D_MODEL])
    u_w, q_w, kv_w, kr_w, gp_w, gm_w = jnp.split(w_in, splits, axis=1)
    kr_w = jnp.pad(kr_w, ((0, 0), (QK_NOPE_DIM, HEAD_PAD - QK_NOPE_DIM - QK_ROPE_DIM)))
    w_in_p = jnp.concatenate([u_w, q_w, kr_w, kv_w, gp_w, gm_w], axis=1).astype(BF16)

    qk_head = QK_NOPE_DIM + QK_ROPE_DIM
    w_q = jnp.pad(w_q_up.reshape(Q_LORA_RANK, N_HEADS, qk_head), ((0, 0), (0, 0), (0, HEAD_PAD - qk_head)))
    w_q = w_q.reshape(Q_LORA_RANK, N_HEADS * HEAD_PAD).astype(BF16)
    kv = w_kv_up.reshape(KV_LORA_RANK, N_HEADS, QK_NOPE_DIM + V_HEAD_DIM)
    w_k = jnp.pad(kv[..., :QK_NOPE_DIM], ((0, 0), (0, 0), (0, HEAD_PAD - QK_NOPE_DIM)))
    w_kv = jnp.concatenate([w_k.reshape(KV_LORA_RANK, N_HEADS * HEAD_PAD),
                            kv[..., QK_NOPE_DIM:].reshape(KV_LORA_RANK, MLA_WIDTH)], axis=1).astype(BF16)

    zeros = lambda n: jnp.zeros((n,), F32)
    q_gain = jnp.tile(jnp.concatenate([q_norm_nope, q_norm_rope, zeros(HEAD_PAD - qk_head)]), N_HEADS)
    k_gain = jnp.tile(jnp.concatenate([k_norm_nope, zeros(HEAD_PAD - QK_NOPE_DIM)]), N_HEADS)
    kr_gain = jnp.concatenate([zeros(QK_NOPE_DIM), k_norm_rope, zeros(HEAD_PAD - qk_head)])

    seg = np.zeros((2 * HEAD_PAD, 2 * HEAD_PAD), np.float32)
    for base in (0, HEAD_PAD):
        seg[base:base + QK_NOPE_DIM, base:base + QK_NOPE_DIM] = 1.0 / QK_NOPE_DIM
        seg[base + QK_NOPE_DIM:base + qk_head, base + QK_NOPE_DIM:base + qk_head] = 1.0 / QK_ROPE_DIM

    inv_freq = ROPE_THETA ** (-jnp.arange(0, QK_ROPE_DIM, 2, dtype=F32) / QK_ROPE_DIM)
    freq = jnp.concatenate([zeros(QK_NOPE_DIM), inv_freq, inv_freq, zeros(HEAD_PAD - qk_head)])

    return {
        "w_in": w_in_p, "pool_grp": pool_grp.astype(BF16), "pool_scale": pool_scale.reshape(1, POOL_WIDTH),
        "w_pool_proj": w_pool_proj.astype(BF16), "q_a_norm": q_a_norm.reshape(1, Q_LORA_RANK), "w_q": w_q,
        "kv_a_norm": kv_a_norm.reshape(1, KV_LORA_RANK), "w_kv": w_kv,
        "q_gain": q_gain.reshape(1, -1), "k_gain": k_gain.reshape(1, -1), "kr_gain": kr_gain.reshape(1, -1),
        "seg": jnp.asarray(seg, BF16), "freq": freq.reshape(1, HEAD_PAD),
    }


def kernel(x, c, positions, w_ada, b_ada, norm_ffn1, w_ffn1_in, w_ffn1_out, norm_mix, w_in, pool_grp,
           pool_scale, w_pool_proj, q_a_norm, w_q_up, kv_a_norm, w_kv_up, q_norm_nope, q_norm_rope,
           k_norm_nope, k_norm_rope, w_mla_proj, w_out, norm_ffn2, w_ffn2_in, w_ffn2_out):
    batch, seq, d = x.shape
    depth = w_ada.shape[0]
    assert d == D_MODEL and seq % TM_FFN == 0 and seq % TM_MIX == 0
    assert seq % TQ == 0 and TQ % TK == 0
    x2d = x.reshape(batch * seq, d)
    pos2d = positions.reshape(batch * seq // HEAD_PAD, HEAD_PAD)
    for l in range(depth):
        mod = _ada(c, w_ada[l], b_ada[l]).reshape(batch, N_MOD_ROWS, d)
        x2d = _ffn(x2d, mod, norm_ffn1[l].reshape(1, d), w_ffn1_in[l].astype(BF16),
                   w_ffn1_out[l].astype(BF16), sub=0, seq=seq)
        w = _prep_mixer_weights(w_in[l], pool_grp[l], pool_scale[l], w_pool_proj[l], q_a_norm[l], w_q_up[l],
                                kv_a_norm[l], w_kv_up[l], q_norm_nope[l], q_norm_rope[l], k_norm_nope[l],
                                k_norm_rope[l])
        q, kt, v, p, g = _mix_in(x2d, mod, pos2d, norm_mix[l].reshape(1, d), w, seq=seq)
        x2d = _attn(q, kt, v, p, g, x2d, mod, w_mla_proj[l].astype(BF16), w_out[l].astype(BF16),
                    batch=batch, seq=seq)
        x2d = _ffn(x2d, mod, norm_ffn2[l].reshape(1, d), w_ffn2_in[l].astype(BF16),
                   w_ffn2_out[l].astype(BF16), sub=2, seq=seq)
    return x2d.reshape(batch, seq, d)
```

```python
import functools
import math

import numpy as np
import jax
import jax.numpy as jnp
from jax import lax
from jax.experimental import pallas as pl
from jax.experimental.pallas import tpu as pltpu

F32 = jnp.float32
BF16 = jnp.bfloat16

D_MODEL = 1024
D_FF = 2816
N_MOD_ROWS = 9
POOL_WINDOWS = (2, 4, 8, 16)
POOL_WIDTH = 512
POOL_GROUP_DIM = 128
POOL_HALO = 16
N_HEADS = 8
QK_NOPE_DIM = 64
QK_ROPE_DIM = 32
V_HEAD_DIM = 64
HEAD_PAD = 128
Q_LORA_RANK = 384
KV_LORA_RANK = 256
MLA_WIDTH = N_HEADS * V_HEAD_DIM
ROPE_THETA = 10000.0
ATTN_SCALE = 1.0 / math.sqrt(QK_NOPE_DIM + QK_ROPE_DIM)
Q_SCALE = ATTN_SCALE * math.log2(math.e)
NORM_EPS = 1e-6
MASK_VALUE = -1e30

TM_FFN = 512
TM_MIX = 512
TQ = 256
TK = 256
S_SLOTS = 4
FF_CHUNK = 256
EARLY_GATE_CHUNKS = 2
ADA_BLOCK = 1152
VMEM_LIMIT = 56 * 1024 * 1024


def _dot(a, b):
    return jnp.dot(a, b, preferred_element_type=F32)


def _rms_norm(x, g):
    return x * lax.rsqrt(jnp.mean(x * x, axis=-1, keepdims=True) + NORM_EPS) * g


def _norm_mod(x, g, shift, scale):
    return _rms_norm(x, g) * (1.0 + scale) + shift


def _ada_kernel(c_ref, w_ref, b_ref, o_ref):
    c = c_ref[...]
    c_act = (c * jax.nn.sigmoid(c)).astype(BF16)
    o_ref[...] = _dot(c_act, w_ref[...].astype(BF16)) + b_ref[...]


def _ada(c, w_ada, b_ada):
    batch = c.shape[0]
    n = w_ada.shape[1]
    return pl.pallas_call(
        _ada_kernel,
        grid=(n // ADA_BLOCK,),
        in_specs=[
            pl.BlockSpec((batch, D_MODEL), lambda i: (0, 0)),
            pl.BlockSpec((D_MODEL, ADA_BLOCK), lambda i: (0, i)),
            pl.BlockSpec((1, ADA_BLOCK), lambda i: (0, i)),
        ],
        out_specs=pl.BlockSpec((batch, ADA_BLOCK), lambda i: (0, i)),
        out_shape=jax.ShapeDtypeStruct((batch, n), F32),
        compiler_params=pltpu.CompilerParams(
            dimension_semantics=("arbitrary",), vmem_limit_bytes=VMEM_LIMIT),
        name="ada",
    )(c, w_ada, b_ada.reshape(1, n))


def _ffn_kernel(x_ref, mod_ref, xn_ref, modn_ref, g_ref, win_ref, wout_ref, o_ref, h_ref, act_ref, *, sub):
    def normed(x, mod):
        return _norm_mod(x, g_ref[...], mod[3 * sub:3 * sub + 1], mod[3 * sub + 1:3 * sub + 2]).astype(BF16)

    def hidden_chunk(c):
        lo = c * FF_CHUNK
        g = _dot(h_ref[...], win_ref[:, lo:lo + FF_CHUNK])
        u = _dot(h_ref[...], win_ref[:, D_FF + lo:D_FF + lo + FF_CHUNK])
        act_ref[:, lo:lo + FF_CHUNK] = (g * jax.nn.sigmoid(g) * u).astype(BF16)

    @pl.when(pl.program_id(0) == 0)
    def _():
        h_ref[...] = normed(x_ref[...], mod_ref[...])
        hidden_chunk(0)

    for c in range(1, D_FF // FF_CHUNK):
        hidden_chunk(c)
    y = _dot(act_ref[...], wout_ref[...])
    gate = mod_ref[...][3 * sub + 2:3 * sub + 3]
    o_ref[...] = x_ref[...] + (0.5 * gate) * y
    h_ref[...] = normed(xn_ref[...], modn_ref[...])
    hidden_chunk(0)


def _const_spec(shape):
    return pl.BlockSpec(shape, lambda *_: (0,) * len(shape), pipeline_mode=pl.Buffered(1))


def _ffn(x2d, mod, g, w_in, w_out, *, sub, seq):
    tokens = x2d.shape[0]
    tiles_per_seq = seq // TM_FFN
    n_tiles = tokens // TM_FFN
    nxt = lambda i: jnp.minimum(i + 1, n_tiles - 1)
    return pl.pallas_call(
        functools.partial(_ffn_kernel, sub=sub),
        grid=(n_tiles,),
        in_specs=[
            pl.BlockSpec((TM_FFN, D_MODEL), lambda i: (i, 0)),
            pl.BlockSpec((pl.Squeezed(), N_MOD_ROWS, D_MODEL), lambda i: (i // tiles_per_seq, 0, 0)),
            pl.BlockSpec((TM_FFN, D_MODEL), lambda i: (nxt(i), 0)),
            pl.BlockSpec((pl.Squeezed(), N_MOD_ROWS, D_MODEL), lambda i: (nxt(i) // tiles_per_seq, 0, 0)),
            _const_spec((1, D_MODEL)),
            _const_spec((D_MODEL, 2 * D_FF)),
            _const_spec((D_FF, D_MODEL)),
        ],
        out_specs=pl.BlockSpec((TM_FFN, D_MODEL), lambda i: (i, 0)),
        out_shape=jax.ShapeDtypeStruct((tokens, D_MODEL), F32),
        scratch_shapes=[pltpu.VMEM((TM_FFN, D_MODEL), BF16), pltpu.VMEM((TM_FFN, D_FF), BF16)],
        compiler_params=pltpu.CompilerParams(
            dimension_semantics=("arbitrary",), vmem_limit_bytes=VMEM_LIMIT),
        name=f"ffn{sub}",
    )(x2d, mod, x2d, mod, g, w_in, w_out)


def _segment_mean_sq(z, seg):
    sq = z * z
    hi = sq.astype(BF16)
    lo = (sq - hi.astype(F32)).astype(BF16)
    width = seg.shape[0]
    outs = []
    for p in range(z.shape[1] // width):
        sl = slice(p * width, (p + 1) * width)
        outs.append(_dot(hi[:, sl], seg) + _dot(lo[:, sl], seg))
    return jnp.concatenate(outs, axis=1)


def _lanes_to_column(p):
    width = p.shape[1]
    diag = (lax.broadcasted_iota(jnp.int32, (width, width), 0)
            == lax.broadcasted_iota(jnp.int32, (width, width), 1))
    cols = [jnp.sum(jnp.where(diag, jnp.broadcast_to(p[a:a + 1, :], (width, width)), 0.0),
                    axis=1, keepdims=True) for a in range(p.shape[0])]
    return jnp.concatenate(cols, axis=0)


def _rope_head(z, cos_t, sin_lo, sin_hi):
    up = pltpu.roll(z, HEAD_PAD - QK_ROPE_DIM // 2, 1)
    down = pltpu.roll(z, QK_ROPE_DIM // 2, 1)
    return z * cos_t + up * sin_lo + down * sin_hi


def _mix_in_kernel(x_ref, mod_ref, pos_ref, gmix_ref, wu_ref, wqk_ref, wkvl_ref, wgp_ref, wgm_ref,
                   pgrp_ref, pscale_ref, wpp_ref,
                   qan_ref, wq_ref, kvan_ref, wkv_ref, qg_ref, kg_ref, krg_ref, seg_ref, freq_ref,
                   q_out, kt_out, v_out, p_out, g_out, ext_ref, *, tiles_per_seq):
    tm = x_ref.shape[0]
    tile_in_seq = pl.program_id(0) % tiles_per_seq
    pair_width = 2 * HEAD_PAD

    @pl.when(pl.program_id(0) == 0)
    def _():
        ext_ref[...] = jnp.zeros(ext_ref.shape, F32)

    seg = seg_ref[...]
    x = x_ref[...]
    mod = mod_ref[...]
    h = _norm_mod(x, gmix_ref[...], mod[3:4], mod[4:5]).astype(BF16)

    qk = _dot(h, wqk_ref[...])
    q_lat, k_rope = qk[:, :Q_LORA_RANK], qk[:, Q_LORA_RANK:]
    kv_lat = _dot(h, wkvl_ref[...])
    u = _dot(h, wu_ref[...])

    lane = lax.broadcasted_iota(jnp.int32, (1, HEAD_PAD), 1)
    ang = _lanes_to_column(pos_ref[...].astype(F32)) * freq_ref[...]
    cos_a, sin_a = jnp.cos(ang), jnp.sin(ang)
    half = QK_ROPE_DIM // 2
    cos_t = jnp.where(lane < QK_NOPE_DIM, 1.0, cos_a)
    sin_lo = jnp.where((lane >= QK_NOPE_DIM) & (lane < QK_NOPE_DIM + half), -sin_a, 0.0)
    sin_hi = jnp.where((lane >= QK_NOPE_DIM + half) & (lane < QK_NOPE_DIM + 2 * half), sin_a, 0.0)

    q = _dot(_rms_norm(q_lat, qan_ref[...]).astype(BF16), wq_ref[...])
    kvn = _rms_norm(kv_lat, kvan_ref[...]).astype(BF16)
    k_nope = _dot(kvn, wkv_ref[:, 0:N_HEADS * HEAD_PAD])
    v_out[...] = _dot(kvn, wkv_ref[:, N_HEADS * HEAD_PAD:]).astype(BF16)

    def mla_gate_chunk(pair):
        lanes = slice(pair * pair_width, (pair + 1) * pair_width)
        g_mla = _dot(h, wgm_ref[:, lanes])
        g_out[:, lanes] = jax.nn.sigmoid(g_mla)

    for pair in range(EARLY_GATE_CHUNKS):
        mla_gate_chunk(pair)

    ext_ref[0:POOL_HALO, :] = jnp.where(tile_in_seq == 0, 0.0, ext_ref[tm:tm + POOL_HALO, :])
    ext_ref[POOL_HALO:POOL_HALO + tm, :] = u
    t_in_seq = tile_in_seq * tm + lax.broadcasted_iota(jnp.int32, (tm, 1), 0)
    pooled = []
    for grp, window in enumerate(POOL_WINDOWS):
        lanes = slice(grp * POOL_GROUP_DIM, (grp + 1) * POOL_GROUP_DIM)
        u_g = u[:, lanes]
        acc = u_g
        for back in range(1, window):
            acc = acc + ext_ref[POOL_HALO - back:POOL_HALO - back + tm, lanes]
        cnt = jnp.minimum(t_in_seq + 1, window).astype(F32)
        pooled.append(_dot((acc / cnt - u_g).astype(BF16), pgrp_ref[grp]))
    pooled = jnp.concatenate(pooled, axis=1) * pscale_ref[...]
    br_pool = _dot(pooled.astype(BF16), wpp_ref[...])

    kr_ms = jnp.sum(k_rope * k_rope, axis=-1, keepdims=True) * (1.0 / QK_ROPE_DIM)
    kr = _rope_head(k_rope * lax.rsqrt(kr_ms + NORM_EPS) * krg_ref[...], cos_t, sin_lo, sin_hi)

    assert D_MODEL == N_HEADS * HEAD_PAD
    for pair in range(N_HEADS // 2):
        lanes = slice(pair * pair_width, (pair + 1) * pair_width)
        g_pool = _dot(h, wgp_ref[:, lanes])
        p_out[:, lanes] = jax.nn.sigmoid(g_pool) * br_pool[:, lanes]
        q_p = q[:, lanes]
        qn = q_p * lax.rsqrt(_segment_mean_sq(q_p, seg) + NORM_EPS) * qg_ref[:, lanes]
        for sub in range(2):
            hd = 2 * pair + sub
            roped = _rope_head(qn[:, sub * HEAD_PAD:(sub + 1) * HEAD_PAD], cos_t, sin_lo, sin_hi)
            q_out[:, hd * HEAD_PAD:(hd + 1) * HEAD_PAD] = (roped * Q_SCALE).astype(BF16)
        if pair + EARLY_GATE_CHUNKS < N_HEADS // 2:
            mla_gate_chunk(pair + EARLY_GATE_CHUNKS)
        k_p = k_nope[:, lanes]
        kn = k_p * lax.rsqrt(_segment_mean_sq(k_p, seg) + NORM_EPS) * kg_ref[:, lanes]
        for sub in range(2):
            hd = 2 * pair + sub
            k_h = kn[:, sub * HEAD_PAD:(sub + 1) * HEAD_PAD] + kr
            kt_out[hd * HEAD_PAD:(hd + 1) * HEAD_PAD, :] = k_h.T.astype(BF16)


def _mix_in(x2d, mod, pos2d, gmix, w, *, seq):
    tokens = x2d.shape[0]
    tm = TM_MIX
    tiles_per_seq = seq // tm
    tok_spec = lambda width: pl.BlockSpec((tm, width), lambda i: (i, 0))
    consts = [gmix, w["w_u"], w["w_qk"], w["w_kvl"], w["w_gp"], w["w_gm"],
              w["pool_grp"], w["pool_scale"], w["w_pool_proj"], w["q_a_norm"], w["w_q"],
              w["kv_a_norm"], w["w_kv"], w["q_gain"], w["k_gain"], w["kr_gain"], w["seg"], w["freq"]]
    return pl.pallas_call(
        functools.partial(_mix_in_kernel, tiles_per_seq=tiles_per_seq),
        grid=(tokens // tm,),
        in_specs=[
            tok_spec(D_MODEL),
            pl.BlockSpec((pl.Squeezed(), N_MOD_ROWS, D_MODEL), lambda i: (i // tiles_per_seq, 0, 0)),
            pl.BlockSpec((pl.Squeezed(), tm // HEAD_PAD, HEAD_PAD), lambda i: (i, 0, 0)),
        ] + [_const_spec(a.shape) for a in consts],
        out_specs=[tok_spec(N_HEADS * HEAD_PAD),
                   pl.BlockSpec((pl.Squeezed(), N_HEADS * HEAD_PAD, tm),
                                lambda i: (i // tiles_per_seq, 0, i % tiles_per_seq)),
                   tok_spec(MLA_WIDTH), tok_spec(D_MODEL), tok_spec(D_MODEL)],
        out_shape=[
            jax.ShapeDtypeStruct((tokens, N_HEADS * HEAD_PAD), BF16),
            jax.ShapeDtypeStruct((tokens // seq, N_HEADS * HEAD_PAD, seq), BF16),
            jax.ShapeDtypeStruct((tokens, MLA_WIDTH), BF16),
            jax.ShapeDtypeStruct((tokens, D_MODEL), F32),
            jax.ShapeDtypeStruct((tokens, D_MODEL), F32),
        ],
        scratch_shapes=[pltpu.VMEM((POOL_HALO + tm, POOL_WIDTH), F32)],
        compiler_params=pltpu.CompilerParams(
            dimension_semantics=("arbitrary",), vmem_limit_bytes=VMEM_LIMIT),
        name="mix_in",
    )(x2d, mod, pos2d.reshape(tokens // tm, tm // HEAD_PAD, HEAD_PAD), *consts)


def _attn_kernel(q_ref, kt_ref, v_ref, p_ref, g_ref, x_ref, mod_ref, wmla_ref, wout_ref, o_ref,
                 s_ref, attn_ref):
    j = pl.program_id(1)
    seq = kt_ref.shape[1]
    row = lax.broadcasted_iota(jnp.int32, (TK, TK), 0)
    col = lax.broadcasted_iota(jnp.int32, (TK, TK), 1)
    causal = col <= row
    lane = lax.broadcasted_iota(jnp.int32, (1, 2 * V_HEAD_DIM), 1)

    def lane_fold(s, op):
        out = s[:, 0:128]
        for t in range(1, TK // 128):
            out = op(out, s[:, t * 128:(t + 1) * 128])
        return out

    def score_pass(unit, hd, block, klen):
        slot = unit % S_SLOTS
        lanes = slice(hd * HEAD_PAD, (hd + 1) * HEAD_PAD)
        q_h = q_ref[block * TK:(block + 1) * TK, lanes]
        n_chunks = klen // TK
        m_run = None
        for c in range(n_chunks):
            cols = slice(c * TK, (c + 1) * TK)
            s = _dot(q_h, kt_ref[lanes, cols])
            if c == n_chunks - 1:
                s = jnp.where(causal, s, MASK_VALUE)
            s_ref[slot, :, cols] = s
            fold = lane_fold(s, jnp.maximum)
            m_run = fold if m_run is None else jnp.maximum(m_run, fold)
        return jnp.max(m_run, axis=-1, keepdims=True)

    def value_pass(unit, hd, klen, m):
        slot = unit % S_SLOTS
        v_lanes = slice((hd // 2) * 128, (hd // 2 + 1) * 128)
        l_run = acc = None
        for c in range(klen // TK):
            cols = slice(c * TK, (c + 1) * TK)
            e = jnp.exp2(s_ref[slot, :, cols] - m)
            fold = lane_fold(e, jnp.add)
            l_run = fold if l_run is None else l_run + fold
            pv = _dot(e.astype(BF16), v_ref[cols, v_lanes])
            acc = pv if acc is None else acc + pv
        return acc / jnp.sum(l_run, axis=-1, keepdims=True)

    def all_units(first_key_end):
        units = [(2 * pair + sub, block) for pair in range(N_HEADS // 2) for block in range(TQ // TK)
                 for sub in range(2)]
        klen = lambda block: first_key_end + block * TK
        row_max, out = {}, {}
        for n in range(len(units) + S_SLOTS):
            done = n - S_SLOTS
            if done >= 0:
                hd, block = units[done]
                out[done] = value_pass(done, hd, klen(block), row_max.pop(done))
                if done % 2 == 1:
                    both = jnp.where(lane < V_HEAD_DIM, out.pop(done - 1), out.pop(done))
                    attn_ref[block * TK:(block + 1) * TK, (hd // 2) * 128:(hd // 2 + 1) * 128] = both.astype(BF16)
            if n < len(units):
                hd, block = units[n]
                row_max[n] = score_pass(n, hd, block, klen(block))

    def project():
        br_mla = _dot(attn_ref[...], wmla_ref[...])
        merged = p_ref[...] + g_ref[...] * br_mla
        gate = mod_ref[...][5:6]
        o_ref[...] = x_ref[...] + gate * _dot(merged.astype(BF16), wout_ref[...])

    for variant in range(seq // TQ):
        @pl.when(j == variant)
        def _(variant=variant):
            all_units(variant * TQ + TK)
            project()


def _attn(q, kt, v, p, g, x2d, mod, w_mla, w_out, *, batch, seq):
    tokens = x2d.shape[0]
    nq = seq // TQ
    tok_spec = lambda width: pl.BlockSpec((TQ, width), lambda b, j: (b * nq + j, 0))
    return pl.pallas_call(
        _attn_kernel,
        grid=(batch, nq),
        in_specs=[
            tok_spec(N_HEADS * HEAD_PAD),
            pl.BlockSpec((pl.Squeezed(), N_HEADS * HEAD_PAD, seq), lambda b, j: (b, 0, 0)),
            pl.BlockSpec((seq, MLA_WIDTH), lambda b, j: (b, 0)),
            tok_spec(D_MODEL), tok_spec(D_MODEL), tok_spec(D_MODEL),
            pl.BlockSpec((pl.Squeezed(), N_MOD_ROWS, D_MODEL), lambda b, j: (b, 0, 0)),
            _const_spec((MLA_WIDTH, D_MODEL)), _const_spec((D_MODEL, D_MODEL)),
        ],
        out_specs=tok_spec(D_MODEL),
        out_shape=jax.ShapeDtypeStruct((tokens, D_MODEL), F32),
        scratch_shapes=[pltpu.VMEM((S_SLOTS, TK, seq), F32), pltpu.VMEM((TQ, MLA_WIDTH), BF16)],
        compiler_params=pltpu.CompilerParams(
            dimension_semantics=("arbitrary", "arbitrary"), vmem_limit_bytes=VMEM_LIMIT),
        name="attn",
    )(q, kt, v, p, g, x2d, mod, w_mla, w_out)


def _prep_mixer_weights(w_in, pool_grp, pool_scale, w_pool_proj, q_a_norm, w_q_up, kv_a_norm, w_kv_up,
                        q_norm_nope, q_norm_rope, k_norm_nope, k_norm_rope):
    splits = np.cumsum([POOL_WIDTH, Q_LORA_RANK, KV_LORA_RANK, QK_ROPE_DIM, D_MODEL])
    u_w, q_w, kv_w, kr_w, gp_w, gm_w = jnp.split(w_in, splits, axis=1)
    kr_w = jnp.pad(kr_w, ((0, 0), (QK_NOPE_DIM, HEAD_PAD - QK_NOPE_DIM - QK_ROPE_DIM)))
    w_qk = jnp.concatenate([q_w, kr_w], axis=1)

    qk_head = QK_NOPE_DIM + QK_ROPE_DIM
    w_q = jnp.pad(w_q_up.reshape(Q_LORA_RANK, N_HEADS, qk_head), ((0, 0), (0, 0), (0, HEAD_PAD - qk_head)))
    w_q = w_q.reshape(Q_LORA_RANK, N_HEADS * HEAD_PAD).astype(BF16)
    kv = w_kv_up.reshape(KV_LORA_RANK, N_HEADS, QK_NOPE_DIM + V_HEAD_DIM)
    w_k = jnp.pad(kv[..., :QK_NOPE_DIM], ((0, 0), (0, 0), (0, HEAD_PAD - QK_NOPE_DIM)))
    w_kv = jnp.concatenate([w_k.reshape(KV_LORA_RANK, N_HEADS * HEAD_PAD),
                            kv[..., QK_NOPE_DIM:].reshape(KV_LORA_RANK, MLA_WIDTH)], axis=1).astype(BF16)

    zeros = lambda n: jnp.zeros((n,), F32)
    q_gain = jnp.tile(jnp.concatenate([q_norm_nope, q_norm_rope, zeros(HEAD_PAD - qk_head)]), N_HEADS)
    k_gain = jnp.tile(jnp.concatenate([k_norm_nope, zeros(HEAD_PAD - QK_NOPE_DIM)]), N_HEADS)
    kr_gain = jnp.concatenate([zeros(QK_NOPE_DIM), k_norm_rope, zeros(HEAD_PAD - qk_head)])

    seg = np.zeros((2 * HEAD_PAD, 2 * HEAD_PAD), np.float32)
    for base in (0, HEAD_PAD):
        seg[base:base + QK_NOPE_DIM, base:base + QK_NOPE_DIM] = 1.0 / QK_NOPE_DIM
        seg[base + QK_NOPE_DIM:base + qk_head, base + QK_NOPE_DIM:base + qk_head] = 1.0 / QK_ROPE_DIM

    inv_freq = ROPE_THETA ** (-jnp.arange(0, QK_ROPE_DIM, 2, dtype=F32) / QK_ROPE_DIM)
    freq = jnp.concatenate([zeros(QK_NOPE_DIM), inv_freq, inv_freq, zeros(HEAD_PAD - qk_head)])

    return {
        "w_u": u_w.astype(BF16), "w_qk": w_qk.astype(BF16), "w_kvl": kv_w.astype(BF16),
        "w_gp": gp_w.astype(BF16), "w_gm": gm_w.astype(BF16), "pool_grp": pool_grp.astype(BF16), "pool_scale": pool_scale.reshape(1, POOL_WIDTH),
        "w_pool_proj": w_pool_proj.astype(BF16), "q_a_norm": q_a_norm.reshape(1, Q_LORA_RANK), "w_q": w_q,
        "kv_a_norm": kv_a_norm.reshape(1, KV_LORA_RANK), "w_kv": w_kv,
        "q_gain": q_gain.reshape(1, -1), "k_gain": k_gain.reshape(1, -1), "kr_gain": kr_gain.reshape(1, -1),
        "seg": jnp.asarray(seg, BF16), "freq": freq.reshape(1, HEAD_PAD),
    }


def kernel(x, c, positions, w_ada, b_ada, norm_ffn1, w_ffn1_in, w_ffn1_out, norm_mix, w_in, pool_grp,
           pool_scale, w_pool_proj, q_a_norm, w_q_up, kv_a_norm, w_kv_up, q_norm_nope, q_norm_rope,
           k_norm_nope, k_norm_rope, w_mla_proj, w_out, norm_ffn2, w_ffn2_in, w_ffn2_out):
    batch, seq, d = x.shape
    depth = w_ada.shape[0]
    assert d == D_MODEL and seq % TM_FFN == 0 and seq % TM_MIX == 0
    assert seq % TQ == 0 and TQ % TK == 0
    x2d = x.reshape(batch * seq, d)
    pos2d = positions.reshape(batch * seq // HEAD_PAD, HEAD_PAD)
    for l in range(depth):
        mod = _ada(c, w_ada[l], b_ada[l]).reshape(batch, N_MOD_ROWS, d)
        x2d = _ffn(x2d, mod, norm_ffn1[l].reshape(1, d), w_ffn1_in[l].astype(BF16),
                   w_ffn1_out[l].astype(BF16), sub=0, seq=seq)
        w = _prep_mixer_weights(w_in[l], pool_grp[l], pool_scale[l], w_pool_proj[l], q_a_norm[l], w_q_up[l],
                                kv_a_norm[l], w_kv_up[l], q_norm_nope[l], q_norm_rope[l], k_norm_nope[l],
                                k_norm_rope[l])
        q, kt, v, p, g = _mix_in(x2d, mod, pos2d, norm_mix[l].reshape(1, d), w, seq=seq)
        x2d = _attn(q, kt, v, p, g, x2d, mod, w_mla_proj[l].astype(BF16), w_out[l].astype(BF16),
                    batch=batch, seq=seq)
        x2d = _ffn(x2d, mod, norm_ffn2[l].reshape(1, d), w_ffn2_in[l].astype(BF16),
                   w_ffn2_out[l].astype(BF16), sub=2, seq=seq)
    return x2d.reshape(batch, seq, d)
```

```python
import functools
import math

import numpy as np
import jax
import jax.numpy as jnp
from jax import lax
from jax.experimental import pallas as pl
from jax.experimental.pallas import tpu as pltpu

F32 = jnp.float32
BF16 = jnp.bfloat16

D_MODEL = 1024
D_FF = 2816
N_MOD_ROWS = 9
POOL_WINDOWS = (2, 4, 8, 16)
POOL_WIDTH = 512
POOL_GROUP_DIM = 128
POOL_HALO = 16
N_HEADS = 8
QK_NOPE_DIM = 64
QK_ROPE_DIM = 32
V_HEAD_DIM = 64
HEAD_PAD = 128
Q_LORA_RANK = 384
KV_LORA_RANK = 256
MLA_WIDTH = N_HEADS * V_HEAD_DIM
ROPE_THETA = 10000.0
ATTN_SCALE = 1.0 / math.sqrt(QK_NOPE_DIM + QK_ROPE_DIM)
Q_SCALE = ATTN_SCALE * math.log2(math.e)
NORM_EPS = 1e-6
MASK_VALUE = -1e30

OFF_U = 0
OFF_QK = POOL_WIDTH
OFF_KV = OFF_QK + Q_LORA_RANK + HEAD_PAD
OFF_GP = OFF_KV + KV_LORA_RANK
OFF_GM = OFF_GP + D_MODEL
IN_WIDTH_PAD = OFF_GM + D_MODEL

TM_FFN = 1024
TM_MIX = 512
TQ = 256
TK = 256
S_SLOTS = 4
FF_CHUNK = 256
EARLY_GATE_CHUNKS = 2
ADA_BLOCK = 1152
VMEM_LIMIT = 56 * 1024 * 1024


def _dot(a, b):
    return jnp.dot(a, b, preferred_element_type=F32)


def _rms_norm(x, g):
    return x * lax.rsqrt(jnp.mean(x * x, axis=-1, keepdims=True) + NORM_EPS) * g


def _norm_mod(x, g, shift, scale):
    return _rms_norm(x, g) * (1.0 + scale) + shift


def _ada_kernel(c_ref, w_ref, b_ref, o_ref):
    c = c_ref[...]
    c_act = (c * jax.nn.sigmoid(c)).astype(BF16)
    o_ref[...] = _dot(c_act, w_ref[...].astype(BF16)) + b_ref[...]


def _ada(c, w_ada, b_ada):
    batch = c.shape[0]
    n = w_ada.shape[1]
    return pl.pallas_call(
        _ada_kernel,
        grid=(n // ADA_BLOCK,),
        in_specs=[
            pl.BlockSpec((batch, D_MODEL), lambda i: (0, 0)),
            pl.BlockSpec((D_MODEL, ADA_BLOCK), lambda i: (0, i)),
            pl.BlockSpec((1, ADA_BLOCK), lambda i: (0, i)),
        ],
        out_specs=pl.BlockSpec((batch, ADA_BLOCK), lambda i: (0, i)),
        out_shape=jax.ShapeDtypeStruct((batch, n), F32),
        compiler_params=pltpu.CompilerParams(
            dimension_semantics=("arbitrary",), vmem_limit_bytes=VMEM_LIMIT),
        name="ada",
    )(c, w_ada, b_ada.reshape(1, n))


def _ffn_kernel(x_ref, mod_ref, xn_ref, modn_ref, g_ref, win_ref, wout_ref, o_ref, h_ref, act_ref, *, sub):
    def normed(x, mod):
        return _norm_mod(x, g_ref[...], mod[3 * sub:3 * sub + 1], mod[3 * sub + 1:3 * sub + 2]).astype(BF16)

    def hidden_chunk(c):
        lo = c * FF_CHUNK
        g = _dot(h_ref[...], win_ref[:, lo:lo + FF_CHUNK])
        u = _dot(h_ref[...], win_ref[:, D_FF + lo:D_FF + lo + FF_CHUNK])
        act_ref[:, lo:lo + FF_CHUNK] = (g * jax.nn.sigmoid(g) * u).astype(BF16)

    @pl.when(pl.program_id(0) == 0)
    def _():
        h_ref[...] = normed(x_ref[...], mod_ref[...])
        hidden_chunk(0)

    for c in range(1, D_FF // FF_CHUNK):
        hidden_chunk(c)
    y = _dot(act_ref[...], wout_ref[...])
    gate = mod_ref[...][3 * sub + 2:3 * sub + 3]
    o_ref[...] = x_ref[...] + (0.5 * gate) * y
    h_ref[...] = normed(xn_ref[...], modn_ref[...])
    hidden_chunk(0)


def _const_spec(shape):
    return pl.BlockSpec(shape, lambda *_: (0,) * len(shape), pipeline_mode=pl.Buffered(1))


def _ffn(x2d, mod, g, w_in, w_out, *, sub, seq):
    tokens = x2d.shape[0]
    tiles_per_seq = seq // TM_FFN
    n_tiles = tokens // TM_FFN
    nxt = lambda i: jnp.minimum(i + 1, n_tiles - 1)
    return pl.pallas_call(
        functools.partial(_ffn_kernel, sub=sub),
        grid=(n_tiles,),
        in_specs=[
            pl.BlockSpec((TM_FFN, D_MODEL), lambda i: (i, 0)),
            pl.BlockSpec((pl.Squeezed(), N_MOD_ROWS, D_MODEL), lambda i: (i // tiles_per_seq, 0, 0)),
            pl.BlockSpec((TM_FFN, D_MODEL), lambda i: (nxt(i), 0)),
            pl.BlockSpec((pl.Squeezed(), N_MOD_ROWS, D_MODEL), lambda i: (nxt(i) // tiles_per_seq, 0, 0)),
            _const_spec((1, D_MODEL)),
            _const_spec((D_MODEL, 2 * D_FF)),
            _const_spec((D_FF, D_MODEL)),
        ],
        out_specs=pl.BlockSpec((TM_FFN, D_MODEL), lambda i: (i, 0)),
        out_shape=jax.ShapeDtypeStruct((tokens, D_MODEL), F32),
        scratch_shapes=[pltpu.VMEM((TM_FFN, D_MODEL), BF16), pltpu.VMEM((TM_FFN, D_FF), BF16)],
        compiler_params=pltpu.CompilerParams(
            dimension_semantics=("arbitrary",), vmem_limit_bytes=VMEM_LIMIT),
        name=f"ffn{sub}",
    )(x2d, mod, x2d, mod, g, w_in, w_out)


def _segment_mean_sq(z, seg):
    sq = z * z
    hi = sq.astype(BF16)
    lo = (sq - hi.astype(F32)).astype(BF16)
    width = seg.shape[0]
    outs = []
    for p in range(z.shape[1] // width):
        sl = slice(p * width, (p + 1) * width)
        outs.append(_dot(hi[:, sl], seg) + _dot(lo[:, sl], seg))
    return jnp.concatenate(outs, axis=1)


def _lanes_to_column(p):
    width = p.shape[1]
    diag = (lax.broadcasted_iota(jnp.int32, (width, width), 0)
            == lax.broadcasted_iota(jnp.int32, (width, width), 1))
    cols = [jnp.sum(jnp.where(diag, jnp.broadcast_to(p[a:a + 1, :], (width, width)), 0.0),
                    axis=1, keepdims=True) for a in range(p.shape[0])]
    return jnp.concatenate(cols, axis=0)


def _rope_head(z, cos_t, sin_lo, sin_hi):
    up = pltpu.roll(z, HEAD_PAD - QK_ROPE_DIM // 2, 1)
    down = pltpu.roll(z, QK_ROPE_DIM // 2, 1)
    return z * cos_t + up * sin_lo + down * sin_hi


def _mix_in_kernel(x_ref, mod_ref, pos_ref, gmix_ref, win_ref, pgrp_ref, pscale_ref, wpp_ref,
                   qan_ref, wq_ref, kvan_ref, wkv_ref, qg_ref, kg_ref, krg_ref, seg_ref, freq_ref,
                   q_out, kt_out, v_out, p_out, g_out, ext_ref, *, tiles_per_seq):
    tm = x_ref.shape[0]
    tile_in_seq = pl.program_id(0) % tiles_per_seq
    pair_width = 2 * HEAD_PAD

    @pl.when(pl.program_id(0) == 0)
    def _():
        ext_ref[...] = jnp.zeros(ext_ref.shape, F32)

    seg = seg_ref[...]
    x = x_ref[...]
    mod = mod_ref[...]
    h = _norm_mod(x, gmix_ref[...], mod[3:4], mod[4:5]).astype(BF16)

    qk = _dot(h, win_ref[:, OFF_QK:OFF_QK + Q_LORA_RANK + HEAD_PAD])
    q_lat, k_rope = qk[:, :Q_LORA_RANK], qk[:, Q_LORA_RANK:]
    kv_lat = _dot(h, win_ref[:, OFF_KV:OFF_KV + KV_LORA_RANK])
    u = _dot(h, win_ref[:, OFF_U:OFF_U + POOL_WIDTH])

    lane = lax.broadcasted_iota(jnp.int32, (1, HEAD_PAD), 1)
    ang = _lanes_to_column(pos_ref[...].astype(F32)) * freq_ref[...]
    cos_a, sin_a = jnp.cos(ang), jnp.sin(ang)
    half = QK_ROPE_DIM // 2
    cos_t = jnp.where(lane < QK_NOPE_DIM, 1.0, cos_a)
    sin_lo = jnp.where((lane >= QK_NOPE_DIM) & (lane < QK_NOPE_DIM + half), -sin_a, 0.0)
    sin_hi = jnp.where((lane >= QK_NOPE_DIM + half) & (lane < QK_NOPE_DIM + 2 * half), sin_a, 0.0)

    q = _dot(_rms_norm(q_lat, qan_ref[...]).astype(BF16), wq_ref[...])
    kvn = _rms_norm(kv_lat, kvan_ref[...]).astype(BF16)
    k_nope = _dot(kvn, wkv_ref[:, 0:N_HEADS * HEAD_PAD])
    v_out[...] = _dot(kvn, wkv_ref[:, N_HEADS * HEAD_PAD:]).astype(BF16)

    def mla_gate_chunk(pair):
        lanes = slice(pair * pair_width, (pair + 1) * pair_width)
        g_mla = _dot(h, win_ref[:, OFF_GM + pair * pair_width:OFF_GM + (pair + 1) * pair_width])
        g_out[:, lanes] = jax.nn.sigmoid(g_mla)

    for pair in range(EARLY_GATE_CHUNKS):
        mla_gate_chunk(pair)

    ext_ref[0:POOL_HALO, :] = jnp.where(tile_in_seq == 0, 0.0, ext_ref[tm:tm + POOL_HALO, :])
    ext_ref[POOL_HALO:POOL_HALO + tm, :] = u
    t_in_seq = tile_in_seq * tm + lax.broadcasted_iota(jnp.int32, (tm, 1), 0)
    pooled = []
    for grp, window in enumerate(POOL_WINDOWS):
        lanes = slice(grp * POOL_GROUP_DIM, (grp + 1) * POOL_GROUP_DIM)
        u_g = u[:, lanes]
        acc = u_g
        for back in range(1, window):
            acc = acc + ext_ref[POOL_HALO - back:POOL_HALO - back + tm, lanes]
        cnt = jnp.minimum(t_in_seq + 1, window).astype(F32)
        pooled.append(_dot((acc / cnt - u_g).astype(BF16), pgrp_ref[grp]))
    pooled = jnp.concatenate(pooled, axis=1) * pscale_ref[...]
    br_pool = _dot(pooled.astype(BF16), wpp_ref[...])

    kr_ms = jnp.sum(k_rope * k_rope, axis=-1, keepdims=True) * (1.0 / QK_ROPE_DIM)
    kr = _rope_head(k_rope * lax.rsqrt(kr_ms + NORM_EPS) * krg_ref[...], cos_t, sin_lo, sin_hi)

    assert D_MODEL == N_HEADS * HEAD_PAD
    for pair in range(N_HEADS // 2):
        lanes = slice(pair * pair_width, (pair + 1) * pair_width)
        g_pool = _dot(h, win_ref[:, OFF_GP + pair * pair_width:OFF_GP + (pair + 1) * pair_width])
        p_out[:, lanes] = jax.nn.sigmoid(g_pool) * br_pool[:, lanes]
        q_p = q[:, lanes]
        qn = q_p * lax.rsqrt(_segment_mean_sq(q_p, seg) + NORM_EPS) * qg_ref[:, lanes]
        for sub in range(2):
            hd = 2 * pair + sub
            roped = _rope_head(qn[:, sub * HEAD_PAD:(sub + 1) * HEAD_PAD], cos_t, sin_lo, sin_hi)
            q_out[:, hd * HEAD_PAD:(hd + 1) * HEAD_PAD] = (roped * Q_SCALE).astype(BF16)
        if pair + EARLY_GATE_CHUNKS < N_HEADS // 2:
            mla_gate_chunk(pair + EARLY_GATE_CHUNKS)
        k_p = k_nope[:, lanes]
        kn = k_p * lax.rsqrt(_segment_mean_sq(k_p, seg) + NORM_EPS) * kg_ref[:, lanes]
        for sub in range(2):
            hd = 2 * pair + sub
            k_h = kn[:, sub * HEAD_PAD:(sub + 1) * HEAD_PAD] + kr
            kt_out[hd * HEAD_PAD:(hd + 1) * HEAD_PAD, :] = k_h.T.astype(BF16)


def _mix_in(x2d, mod, pos2d, gmix, w, *, seq):
    tokens = x2d.shape[0]
    tm = TM_MIX
    tiles_per_seq = seq // tm
    tok_spec = lambda width: pl.BlockSpec((tm, width), lambda i: (i, 0))
    consts = [gmix, w["w_in"], w["pool_grp"], w["pool_scale"], w["w_pool_proj"], w["q_a_norm"], w["w_q"],
              w["kv_a_norm"], w["w_kv"], w["q_gain"], w["k_gain"], w["kr_gain"], w["seg"], w["freq"]]
    return pl.pallas_call(
        functools.partial(_mix_in_kernel, tiles_per_seq=tiles_per_seq),
        grid=(tokens // tm,),
        in_specs=[
            tok_spec(D_MODEL),
            pl.BlockSpec((pl.Squeezed(), N_MOD_ROWS, D_MODEL), lambda i: (i // tiles_per_seq, 0, 0)),
            pl.BlockSpec((pl.Squeezed(), tm // HEAD_PAD, HEAD_PAD), lambda i: (i, 0, 0)),
        ] + [_const_spec(a.shape) for a in consts],
        out_specs=[tok_spec(N_HEADS * HEAD_PAD),
                   pl.BlockSpec((pl.Squeezed(), N_HEADS * HEAD_PAD, tm),
                                lambda i: (i // tiles_per_seq, 0, i % tiles_per_seq)),
                   tok_spec(MLA_WIDTH), tok_spec(D_MODEL), tok_spec(D_MODEL)],
        out_shape=[
            jax.ShapeDtypeStruct((tokens, N_HEADS * HEAD_PAD), BF16),
            jax.ShapeDtypeStruct((tokens // seq, N_HEADS * HEAD_PAD, seq), BF16),
            jax.ShapeDtypeStruct((tokens, MLA_WIDTH), BF16),
            jax.ShapeDtypeStruct((tokens, D_MODEL), F32),
            jax.ShapeDtypeStruct((tokens, D_MODEL), F32),
        ],
        scratch_shapes=[pltpu.VMEM((POOL_HALO + tm, POOL_WIDTH), F32)],
        compiler_params=pltpu.CompilerParams(
            dimension_semantics=("arbitrary",), vmem_limit_bytes=VMEM_LIMIT),
        name="mix_in",
    )(x2d, mod, pos2d.reshape(tokens // tm, tm // HEAD_PAD, HEAD_PAD), *consts)


def _attn_kernel(q_ref, kt_ref, v_ref, p_ref, g_ref, x_ref, mod_ref, wmla_ref, wout_ref, o_ref,
                 s_ref, attn_ref):
    j = pl.program_id(1)
    seq = kt_ref.shape[1]
    row = lax.broadcasted_iota(jnp.int32, (TK, TK), 0)
    col = lax.broadcasted_iota(jnp.int32, (TK, TK), 1)
    causal = col <= row
    lane = lax.broadcasted_iota(jnp.int32, (1, 2 * V_HEAD_DIM), 1)

    def lane_fold(s, op):
        out = s[:, 0:128]
        for t in range(1, TK // 128):
            out = op(out, s[:, t * 128:(t + 1) * 128])
        return out

    def score_pass(unit, hd, block, klen):
        slot = unit % S_SLOTS
        lanes = slice(hd * HEAD_PAD, (hd + 1) * HEAD_PAD)
        q_h = q_ref[block * TK:(block + 1) * TK, lanes]
        n_chunks = klen // TK
        m_run = None
        for c in range(n_chunks):
            cols = slice(c * TK, (c + 1) * TK)
            s = _dot(q_h, kt_ref[lanes, cols])
            if c == n_chunks - 1:
                s = jnp.where(causal, s, MASK_VALUE)
            s_ref[slot, :, cols] = s
            fold = lane_fold(s, jnp.maximum)
            m_run = fold if m_run is None else jnp.maximum(m_run, fold)
        return jnp.max(m_run, axis=-1, keepdims=True)

    def value_pass(unit, hd, klen, m):
        slot = unit % S_SLOTS
        v_lanes = slice((hd // 2) * 128, (hd // 2 + 1) * 128)
        l_run = acc = None
        for c in range(klen // TK):
            cols = slice(c * TK, (c + 1) * TK)
            e = jnp.exp2(s_ref[slot, :, cols] - m)
            fold = lane_fold(e, jnp.add)
            l_run = fold if l_run is None else l_run + fold
            pv = _dot(e.astype(BF16), v_ref[cols, v_lanes])
            acc = pv if acc is None else acc + pv
        return acc / jnp.sum(l_run, axis=-1, keepdims=True)

    def all_units(first_key_end):
        units = [(2 * pair + sub, block) for pair in range(N_HEADS // 2) for block in range(TQ // TK)
                 for sub in range(2)]
        klen = lambda block: first_key_end + block * TK
        row_max, out = {}, {}
        for n in range(len(units) + S_SLOTS):
            done = n - S_SLOTS
            if done >= 0:
                hd, block = units[done]
                out[done] = value_pass(done, hd, klen(block), row_max.pop(done))
                if done % 2 == 1:
                    both = jnp.where(lane < V_HEAD_DIM, out.pop(done - 1), out.pop(done))
                    attn_ref[block * TK:(block + 1) * TK, (hd // 2) * 128:(hd // 2 + 1) * 128] = both.astype(BF16)
            if n < len(units):
                hd, block = units[n]
                row_max[n] = score_pass(n, hd, block, klen(block))

    def project():
        br_mla = _dot(attn_ref[...], wmla_ref[...])
        merged = p_ref[...] + g_ref[...] * br_mla
        gate = mod_ref[...][5:6]
        o_ref[...] = x_ref[...] + gate * _dot(merged.astype(BF16), wout_ref[...])

    for variant in range(seq // TQ):
        @pl.when(j == variant)
        def _(variant=variant):
            all_units(variant * TQ + TK)
            project()


def _attn(q, kt, v, p, g, x2d, mod, w_mla, w_out, *, batch, seq):
    tokens = x2d.shape[0]
    nq = seq // TQ
    tok_spec = lambda width: pl.BlockSpec((TQ, width), lambda b, j: (b * nq + j, 0))
    return pl.pallas_call(
        _attn_kernel,
        grid=(batch, nq),
        in_specs=[
            tok_spec(N_HEADS * HEAD_PAD),
            pl.BlockSpec((pl.Squeezed(), N_HEADS * HEAD_PAD, seq), lambda b, j: (b, 0, 0)),
            pl.BlockSpec((seq, MLA_WIDTH), lambda b, j: (b, 0)),
            tok_spec(D_MODEL), tok_spec(D_MODEL), tok_spec(D_MODEL),
            pl.BlockSpec((pl.Squeezed(), N_MOD_ROWS, D_MODEL), lambda b, j: (b, 0, 0)),
            _const_spec((MLA_WIDTH, D_MODEL)), _const_spec((D_MODEL, D_MODEL)),
        ],
        out_specs=tok_spec(D_MODEL),
        out_shape=jax.ShapeDtypeStruct((tokens, D_MODEL), F32),
        scratch_shapes=[pltpu.VMEM((S_SLOTS, TK, seq), F32), pltpu.VMEM((TQ, MLA_WIDTH), BF16)],
        compiler_params=pltpu.CompilerParams(
            dimension_semantics=("arbitrary", "arbitrary"), vmem_limit_bytes=VMEM_LIMIT),
        name="attn",
    )(q, kt, v, p, g, x2d, mod, w_mla, w_out)


def _prep_mixer_weights(w_in, pool_grp, pool_scale, w_pool_proj, q_a_norm, w_q_up, kv_a_norm, w_kv_up,
                        q_norm_nope, q_norm_rope, k_norm_nope, k_norm_rope):
    splits = np.cumsum([POOL_WIDTH, Q_LORA_RANK, KV_LORA_RANK, QK_ROPE_DIM, D_MODEL])
    u_w, q_w, kv_w, kr_w, gp_w, gm_w = jnp.split(w_in, splits, axis=1)
    kr_w = jnp.pad(kr_w, ((0, 0), (QK_NOPE_DIM, HEAD_PAD - QK_NOPE_DIM - QK_ROPE_DIM)))
    w_in_p = jnp.concatenate([u_w, q_w, kr_w, kv_w, gp_w, gm_w], axis=1).astype(BF16)

    qk_head = QK_NOPE_DIM + QK_ROPE_DIM
    w_q = jnp.pad(w_q_up.reshape(Q_LORA_RANK, N_HEADS, qk_head), ((0, 0), (0, 0), (0, HEAD_PAD - qk_head)))
    w_q = w_q.reshape(Q_LORA_RANK, N_HEADS * HEAD_PAD).astype(BF16)
    kv = w_kv_up.reshape(KV_LORA_RANK, N_HEADS, QK_NOPE_DIM + V_HEAD_DIM)
    w_k = jnp.pad(kv[..., :QK_NOPE_DIM], ((0, 0), (0, 0), (0, HEAD_PAD - QK_NOPE_DIM)))
    w_kv = jnp.concatenate([w_k.reshape(KV_LORA_RANK, N_HEADS * HEAD_PAD),
                            kv[..., QK_NOPE_DIM:].reshape(KV_LORA_RANK, MLA_WIDTH)], axis=1).astype(BF16)

    zeros = lambda n: jnp.zeros((n,), F32)
    q_gain = jnp.tile(jnp.concatenate([q_norm_nope, q_norm_rope, zeros(HEAD_PAD - qk_head)]), N_HEADS)
    k_gain = jnp.tile(jnp.concatenate([k_norm_nope, zeros(HEAD_PAD - QK_NOPE_DIM)]), N_HEADS)
    kr_gain = jnp.concatenate([zeros(QK_NOPE_DIM), k_norm_rope, zeros(HEAD_PAD - qk_head)])

    seg = np.zeros((2 * HEAD_PAD, 2 * HEAD_PAD), np.float32)
    for base in (0, HEAD_PAD):
        seg[base:base + QK_NOPE_DIM, base:base + QK_NOPE_DIM] = 1.0 / QK_NOPE_DIM
        seg[base + QK_NOPE_DIM:base + qk_head, base + QK_NOPE_DIM:base + qk_head] = 1.0 / QK_ROPE_DIM

    inv_freq = ROPE_THETA ** (-jnp.arange(0, QK_ROPE_DIM, 2, dtype=F32) / QK_ROPE_DIM)
    freq = jnp.concatenate([zeros(QK_NOPE_DIM), inv_freq, inv_freq, zeros(HEAD_PAD - qk_head)])

    return {
        "w_in": w_in_p, "pool_grp": pool_grp.astype(BF16), "pool_scale": pool_scale.reshape(1, POOL_WIDTH),
        "w_pool_proj": w_pool_proj.astype(BF16), "q_a_norm": q_a_norm.reshape(1, Q_LORA_RANK), "w_q": w_q,
        "kv_a_norm": kv_a_norm.reshape(1, KV_LORA_RANK), "w_kv": w_kv,
        "q_gain": q_gain.reshape(1, -1), "k_gain": k_gain.reshape(1, -1), "kr_gain": kr_gain.reshape(1, -1),
        "seg": jnp.asarray(seg, BF16), "freq": freq.reshape(1, HEAD_PAD),
    }


def kernel(x, c, positions, w_ada, b_ada, norm_ffn1, w_ffn1_in, w_ffn1_out, norm_mix, w_in, pool_grp,
           pool_scale, w_pool_proj, q_a_norm, w_q_up, kv_a_norm, w_kv_up, q_norm_nope, q_norm_rope,
           k_norm_nope, k_norm_rope, w_mla_proj, w_out, norm_ffn2, w_ffn2_in, w_ffn2_out):
    batch, seq, d = x.shape
    depth = w_ada.shape[0]
    assert d == D_MODEL and seq % TM_FFN == 0 and seq % TM_MIX == 0
    assert seq % TQ == 0 and TQ % TK == 0
    x2d = x.reshape(batch * seq, d)
    pos2d = positions.reshape(batch * seq // HEAD_PAD, HEAD_PAD)
    for l in range(depth):
        mod = _ada(c, w_ada[l], b_ada[l]).reshape(batch, N_MOD_ROWS, d)
        x2d = _ffn(x2d, mod, norm_ffn1[l].reshape(1, d), w_ffn1_in[l].astype(BF16),
                   w_ffn1_out[l].astype(BF16), sub=0, seq=seq)
        w = _prep_mixer_weights(w_in[l], pool_grp[l], pool_scale[l], w_pool_proj[l], q_a_norm[l], w_q_up[l],
                                kv_a_norm[l], w_kv_up[l], q_norm_nope[l], q_norm_rope[l], k_norm_nope[l],
                                k_norm_rope[l])
        q, kt, v, p, g = _mix_in(x2d, mod, pos2d, norm_mix[l].reshape(1, d), w, seq=seq)
        x2d = _attn(q, kt, v, p, g, x2d, mod, w_mla_proj[l].astype(BF16), w_out[l].astype(BF16),
                    batch=batch, seq=seq)
        x2d = _ffn(x2d, mod, norm_ffn2[l].reshape(1, d), w_ffn2_in[l].astype(BF16),
                   w_ffn2_out[l].astype(BF16), sub=2, seq=seq)
    return x2d.reshape(batch, seq, d)
```

```python
import functools
import math

import numpy as np
import jax
import jax.numpy as jnp
from jax import lax
from jax.experimental import pallas as pl
from jax.experimental.pallas import tpu as pltpu

F32 = jnp.float32
BF16 = jnp.bfloat16

D_MODEL = 1024
D_FF = 2816
N_MOD_ROWS = 9
POOL_WINDOWS = (2, 4, 8, 16)
POOL_WIDTH = 512
POOL_GROUP_DIM = 128
POOL_HALO = 16
N_HEADS = 8
QK_NOPE_DIM = 64
QK_ROPE_DIM = 32
V_HEAD_DIM = 64
HEAD_PAD = 128
Q_LORA_RANK = 384
KV_LORA_RANK = 256
MLA_WIDTH = N_HEADS * V_HEAD_DIM
ROPE_THETA = 10000.0
ATTN_SCALE = 1.0 / math.sqrt(QK_NOPE_DIM + QK_ROPE_DIM)
Q_SCALE = ATTN_SCALE * math.log2(math.e)
NORM_EPS = 1e-6
MASK_VALUE = -1e30

OFF_U = 0
OFF_QK = POOL_WIDTH
OFF_KV = OFF_QK + Q_LORA_RANK + HEAD_PAD
OFF_GP = OFF_KV + KV_LORA_RANK
OFF_GM = OFF_GP + D_MODEL
IN_WIDTH_PAD = OFF_GM + D_MODEL

TM_FFN = 1024
TM_MIX = 512
TQ = 256
TK = 256
S_SLOTS = 3
FF_CHUNK = 256
EARLY_GATE_CHUNKS = 2
ADA_BLOCK = 1152
VMEM_LIMIT = 56 * 1024 * 1024


def _dot(a, b):
    return jnp.dot(a, b, preferred_element_type=F32)


def _rms_norm(x, g):
    return x * lax.rsqrt(jnp.mean(x * x, axis=-1, keepdims=True) + NORM_EPS) * g


def _norm_mod(x, g, shift, scale):
    return _rms_norm(x, g) * (1.0 + scale) + shift


def _ada_kernel(c_ref, w_ref, b_ref, o_ref):
    c = c_ref[...]
    c_act = (c * jax.nn.sigmoid(c)).astype(BF16)
    o_ref[...] = _dot(c_act, w_ref[...].astype(BF16)) + b_ref[...]


def _ada(c, w_ada, b_ada):
    batch = c.shape[0]
    n = w_ada.shape[1]
    return pl.pallas_call(
        _ada_kernel,
        grid=(n // ADA_BLOCK,),
        in_specs=[
            pl.BlockSpec((batch, D_MODEL), lambda i: (0, 0)),
            pl.BlockSpec((D_MODEL, ADA_BLOCK), lambda i: (0, i)),
            pl.BlockSpec((1, ADA_BLOCK), lambda i: (0, i)),
        ],
        out_specs=pl.BlockSpec((batch, ADA_BLOCK), lambda i: (0, i)),
        out_shape=jax.ShapeDtypeStruct((batch, n), F32),
        compiler_params=pltpu.CompilerParams(
            dimension_semantics=("arbitrary",), vmem_limit_bytes=VMEM_LIMIT),
        name="ada",
    )(c, w_ada, b_ada.reshape(1, n))


def _ffn_kernel(x_ref, mod_ref, xn_ref, modn_ref, g_ref, win_ref, wout_ref, o_ref, h_ref, act_ref, *, sub):
    def normed(x, mod):
        return _norm_mod(x, g_ref[...], mod[3 * sub:3 * sub + 1], mod[3 * sub + 1:3 * sub + 2]).astype(BF16)

    def hidden_chunk(c):
        lo = c * FF_CHUNK
        g = _dot(h_ref[...], win_ref[:, lo:lo + FF_CHUNK])
        u = _dot(h_ref[...], win_ref[:, D_FF + lo:D_FF + lo + FF_CHUNK])
        act_ref[:, lo:lo + FF_CHUNK] = (g * jax.nn.sigmoid(g) * u).astype(BF16)

    @pl.when(pl.program_id(0) == 0)
    def _():
        h_ref[...] = normed(x_ref[...], mod_ref[...])
        hidden_chunk(0)

    for c in range(1, D_FF // FF_CHUNK):
        hidden_chunk(c)
    y = _dot(act_ref[...], wout_ref[...])
    gate = mod_ref[...][3 * sub + 2:3 * sub + 3]
    o_ref[...] = x_ref[...] + (0.5 * gate) * y
    h_ref[...] = normed(xn_ref[...], modn_ref[...])
    hidden_chunk(0)


def _const_spec(shape):
    return pl.BlockSpec(shape, lambda *_: (0,) * len(shape), pipeline_mode=pl.Buffered(1))


def _ffn(x2d, mod, g, w_in, w_out, *, sub, seq):
    tokens = x2d.shape[0]
    tiles_per_seq = seq // TM_FFN
    n_tiles = tokens // TM_FFN
    nxt = lambda i: jnp.minimum(i + 1, n_tiles - 1)
    return pl.pallas_call(
        functools.partial(_ffn_kernel, sub=sub),
        grid=(n_tiles,),
        in_specs=[
            pl.BlockSpec((TM_FFN, D_MODEL), lambda i: (i, 0)),
            pl.BlockSpec((pl.Squeezed(), N_MOD_ROWS, D_MODEL), lambda i: (i // tiles_per_seq, 0, 0)),
            pl.BlockSpec((TM_FFN, D_MODEL), lambda i: (nxt(i), 0)),
            pl.BlockSpec((pl.Squeezed(), N_MOD_ROWS, D_MODEL), lambda i: (nxt(i) // tiles_per_seq, 0, 0)),
            _const_spec((1, D_MODEL)),
            _const_spec((D_MODEL, 2 * D_FF)),
            _const_spec((D_FF, D_MODEL)),
        ],
        out_specs=pl.BlockSpec((TM_FFN, D_MODEL), lambda i: (i, 0)),
        out_shape=jax.ShapeDtypeStruct((tokens, D_MODEL), F32),
        scratch_shapes=[pltpu.VMEM((TM_FFN, D_MODEL), BF16), pltpu.VMEM((TM_FFN, D_FF), BF16)],
        compiler_params=pltpu.CompilerParams(
            dimension_semantics=("arbitrary",), vmem_limit_bytes=VMEM_LIMIT),
        name=f"ffn{sub}",
    )(x2d, mod, x2d, mod, g, w_in, w_out)


def _segment_mean_sq(z, seg):
    sq = z * z
    hi = sq.astype(BF16)
    lo = (sq - hi.astype(F32)).astype(BF16)
    width = seg.shape[0]
    outs = []
    for p in range(z.shape[1] // width):
        sl = slice(p * width, (p + 1) * width)
        outs.append(_dot(hi[:, sl], seg) + _dot(lo[:, sl], seg))
    return jnp.concatenate(outs, axis=1)


def _lanes_to_column(p):
    width = p.shape[1]
    diag = (lax.broadcasted_iota(jnp.int32, (width, width), 0)
            == lax.broadcasted_iota(jnp.int32, (width, width), 1))
    cols = [jnp.sum(jnp.where(diag, jnp.broadcast_to(p[a:a + 1, :], (width, width)), 0.0),
                    axis=1, keepdims=True) for a in range(p.shape[0])]
    return jnp.concatenate(cols, axis=0)


def _rope_head(z, cos_t, sin_lo, sin_hi):
    up = pltpu.roll(z, HEAD_PAD - QK_ROPE_DIM // 2, 1)
    down = pltpu.roll(z, QK_ROPE_DIM // 2, 1)
    return z * cos_t + up * sin_lo + down * sin_hi


def _mix_in_kernel(x_ref, mod_ref, pos_ref, gmix_ref, win_ref, pgrp_ref, pscale_ref, wpp_ref,
                   qan_ref, wq_ref, kvan_ref, wkv_ref, qg_ref, kg_ref, krg_ref, seg_ref, freq_ref,
                   q_out, kt_out, v_out, p_out, g_out, ext_ref, *, tiles_per_seq):
    tm = x_ref.shape[0]
    tile_in_seq = pl.program_id(0) % tiles_per_seq
    pair_width = 2 * HEAD_PAD

    @pl.when(pl.program_id(0) == 0)
    def _():
        ext_ref[...] = jnp.zeros(ext_ref.shape, F32)

    seg = seg_ref[...]
    x = x_ref[...]
    mod = mod_ref[...]
    h = _norm_mod(x, gmix_ref[...], mod[3:4], mod[4:5]).astype(BF16)

    qk = _dot(h, win_ref[:, OFF_QK:OFF_QK + Q_LORA_RANK + HEAD_PAD])
    q_lat, k_rope = qk[:, :Q_LORA_RANK], qk[:, Q_LORA_RANK:]
    kv_lat = _dot(h, win_ref[:, OFF_KV:OFF_KV + KV_LORA_RANK])
    u = _dot(h, win_ref[:, OFF_U:OFF_U + POOL_WIDTH])

    lane = lax.broadcasted_iota(jnp.int32, (1, HEAD_PAD), 1)
    ang = _lanes_to_column(pos_ref[...].astype(F32)) * freq_ref[...]
    cos_a, sin_a = jnp.cos(ang), jnp.sin(ang)
    half = QK_ROPE_DIM // 2
    cos_t = jnp.where(lane < QK_NOPE_DIM, 1.0, cos_a)
    sin_lo = jnp.where((lane >= QK_NOPE_DIM) & (lane < QK_NOPE_DIM + half), -sin_a, 0.0)
    sin_hi = jnp.where((lane >= QK_NOPE_DIM + half) & (lane < QK_NOPE_DIM + 2 * half), sin_a, 0.0)

    q = _dot(_rms_norm(q_lat, qan_ref[...]).astype(BF16), wq_ref[...])
    kvn = _rms_norm(kv_lat, kvan_ref[...]).astype(BF16)
    k_nope = _dot(kvn, wkv_ref[:, 0:N_HEADS * HEAD_PAD])
    v_out[...] = _dot(kvn, wkv_ref[:, N_HEADS * HEAD_PAD:]).astype(BF16)

    def mla_gate_chunk(pair):
        lanes = slice(pair * pair_width, (pair + 1) * pair_width)
        g_mla = _dot(h, win_ref[:, OFF_GM + pair * pair_width:OFF_GM + (pair + 1) * pair_width])
        g_out[:, lanes] = jax.nn.sigmoid(g_mla)

    for pair in range(EARLY_GATE_CHUNKS):
        mla_gate_chunk(pair)

    ext_ref[0:POOL_HALO, :] = jnp.where(tile_in_seq == 0, 0.0, ext_ref[tm:tm + POOL_HALO, :])
    ext_ref[POOL_HALO:POOL_HALO + tm, :] = u
    t_in_seq = tile_in_seq * tm + lax.broadcasted_iota(jnp.int32, (tm, 1), 0)
    pooled = []
    for grp, window in enumerate(POOL_WINDOWS):
        lanes = slice(grp * POOL_GROUP_DIM, (grp + 1) * POOL_GROUP_DIM)
        u_g = u[:, lanes]
        acc = u_g
        for back in range(1, window):
            acc = acc + ext_ref[POOL_HALO - back:POOL_HALO - back + tm, lanes]
        cnt = jnp.minimum(t_in_seq + 1, window).astype(F32)
        pooled.append(_dot((acc / cnt - u_g).astype(BF16), pgrp_ref[grp]))
    pooled = jnp.concatenate(pooled, axis=1) * pscale_ref[...]
    br_pool = _dot(pooled.astype(BF16), wpp_ref[...])

    kr_ms = jnp.sum(k_rope * k_rope, axis=-1, keepdims=True) * (1.0 / QK_ROPE_DIM)
    kr = _rope_head(k_rope * lax.rsqrt(kr_ms + NORM_EPS) * krg_ref[...], cos_t, sin_lo, sin_hi)

    assert D_MODEL == N_HEADS * HEAD_PAD
    for pair in range(N_HEADS // 2):
        lanes = slice(pair * pair_width, (pair + 1) * pair_width)
        g_pool = _dot(h, win_ref[:, OFF_GP + pair * pair_width:OFF_GP + (pair + 1) * pair_width])
        p_out[:, lanes] = jax.nn.sigmoid(g_pool) * br_pool[:, lanes]
        q_p = q[:, lanes]
        qn = q_p * lax.rsqrt(_segment_mean_sq(q_p, seg) + NORM_EPS) * qg_ref[:, lanes]
        for sub in range(2):
            hd = 2 * pair + sub
            roped = _rope_head(qn[:, sub * HEAD_PAD:(sub + 1) * HEAD_PAD], cos_t, sin_lo, sin_hi)
            q_out[:, hd * HEAD_PAD:(hd + 1) * HEAD_PAD] = (roped * Q_SCALE).astype(BF16)
        if pair + EARLY_GATE_CHUNKS < N_HEADS // 2:
            mla_gate_chunk(pair + EARLY_GATE_CHUNKS)
        k_p = k_nope[:, lanes]
        kn = k_p * lax.rsqrt(_segment_mean_sq(k_p, seg) + NORM_EPS) * kg_ref[:, lanes]
        for sub in range(2):
            hd = 2 * pair + sub
            k_h = kn[:, sub * HEAD_PAD:(sub + 1) * HEAD_PAD] + kr
            kt_out[hd * HEAD_PAD:(hd + 1) * HEAD_PAD, :] = k_h.T.astype(BF16)


def _mix_in(x2d, mod, pos2d, gmix, w, *, seq):
    tokens = x2d.shape[0]
    tm = TM_MIX
    tiles_per_seq = seq // tm
    tok_spec = lambda width: pl.BlockSpec((tm, width), lambda i: (i, 0))
    consts = [gmix, w["w_in"], w["pool_grp"], w["pool_scale"], w["w_pool_proj"], w["q_a_norm"], w["w_q"],
              w["kv_a_norm"], w["w_kv"], w["q_gain"], w["k_gain"], w["kr_gain"], w["seg"], w["freq"]]
    return pl.pallas_call(
        functools.partial(_mix_in_kernel, tiles_per_seq=tiles_per_seq),
        grid=(tokens // tm,),
        in_specs=[
            tok_spec(D_MODEL),
            pl.BlockSpec((pl.Squeezed(), N_MOD_ROWS, D_MODEL), lambda i: (i // tiles_per_seq, 0, 0)),
            pl.BlockSpec((pl.Squeezed(), tm // HEAD_PAD, HEAD_PAD), lambda i: (i, 0, 0)),
        ] + [_const_spec(a.shape) for a in consts],
        out_specs=[tok_spec(N_HEADS * HEAD_PAD),
                   pl.BlockSpec((pl.Squeezed(), N_HEADS * HEAD_PAD, tm),
                                lambda i: (i // tiles_per_seq, 0, i % tiles_per_seq)),
                   tok_spec(MLA_WIDTH), tok_spec(D_MODEL), tok_spec(D_MODEL)],
        out_shape=[
            jax.ShapeDtypeStruct((tokens, N_HEADS * HEAD_PAD), BF16),
            jax.ShapeDtypeStruct((tokens // seq, N_HEADS * HEAD_PAD, seq), BF16),
            jax.ShapeDtypeStruct((tokens, MLA_WIDTH), BF16),
            jax.ShapeDtypeStruct((tokens, D_MODEL), F32),
            jax.ShapeDtypeStruct((tokens, D_MODEL), F32),
        ],
        scratch_shapes=[pltpu.VMEM((POOL_HALO + tm, POOL_WIDTH), F32)],
        compiler_params=pltpu.CompilerParams(
            dimension_semantics=("arbitrary",), vmem_limit_bytes=VMEM_LIMIT),
        name="mix_in",
    )(x2d, mod, pos2d.reshape(tokens // tm, tm // HEAD_PAD, HEAD_PAD), *consts)


def _attn_kernel(q_ref, kt_ref, v_ref, p_ref, g_ref, x_ref, mod_ref, wmla_ref, wout_ref, o_ref,
                 s_ref, attn_ref):
    j = pl.program_id(1)
    seq = kt_ref.shape[1]
    row = lax.broadcasted_iota(jnp.int32, (TK, TK), 0)
    col = lax.broadcasted_iota(jnp.int32, (TK, TK), 1)
    causal = col <= row
    lane = lax.broadcasted_iota(jnp.int32, (1, 2 * V_HEAD_DIM), 1)

    def lane_fold(s, op):
        out = s[:, 0:128]
        for t in range(1, TK // 128):
            out = op(out, s[:, t * 128:(t + 1) * 128])
        return out

    def score_pass(unit, hd, block, klen):
        slot = unit % S_SLOTS
        lanes = slice(hd * HEAD_PAD, (hd + 1) * HEAD_PAD)
        q_h = q_ref[block * TK:(block + 1) * TK, lanes]
        n_chunks = klen // TK
        m_run = None
        for c in range(n_chunks):
            cols = slice(c * TK, (c + 1) * TK)
            s = _dot(q_h, kt_ref[lanes, cols])
            if c == n_chunks - 1:
                s = jnp.where(causal, s, MASK_VALUE)
            s_ref[slot, :, cols] = s
            fold = lane_fold(s, jnp.maximum)
            m_run = fold if m_run is None else jnp.maximum(m_run, fold)
        return jnp.max(m_run, axis=-1, keepdims=True)

    def value_pass(unit, hd, klen, m):
        slot = unit % S_SLOTS
        v_lanes = slice((hd // 2) * 128, (hd // 2 + 1) * 128)
        l_run = acc = None
        for c in range(klen // TK):
            cols = slice(c * TK, (c + 1) * TK)
            e = jnp.exp2(s_ref[slot, :, cols] - m)
            fold = lane_fold(e, jnp.add)
            l_run = fold if l_run is None else l_run + fold
            pv = _dot(e.astype(BF16), v_ref[cols, v_lanes])
            acc = pv if acc is None else acc + pv
        return acc / jnp.sum(l_run, axis=-1, keepdims=True)

    def all_units(first_key_end):
        units = [(2 * pair + sub, block) for pair in range(N_HEADS // 2) for block in range(TQ // TK)
                 for sub in range(2)]
        klen = lambda block: first_key_end + block * TK
        row_max, out = {}, {}
        for n in range(len(units) + S_SLOTS):
            done = n - S_SLOTS
            if done >= 0:
                hd, block = units[done]
                out[done] = value_pass(done, hd, klen(block), row_max.pop(done))
                if done % 2 == 1:
                    both = jnp.where(lane < V_HEAD_DIM, out.pop(done - 1), out.pop(done))
                    attn_ref[block * TK:(block + 1) * TK, (hd // 2) * 128:(hd // 2 + 1) * 128] = both.astype(BF16)
            if n < len(units):
                hd, block = units[n]
                row_max[n] = score_pass(n, hd, block, klen(block))

    def project():
        br_mla = _dot(attn_ref[...], wmla_ref[...])
        merged = p_ref[...] + g_ref[...] * br_mla
        gate = mod_ref[...][5:6]
        o_ref[...] = x_ref[...] + gate * _dot(merged.astype(BF16), wout_ref[...])

    for variant in range(seq // TQ):
        @pl.when(j == variant)
        def _(variant=variant):
            all_units(variant * TQ + TK)
            project()


def _attn(q, kt, v, p, g, x2d, mod, w_mla, w_out, *, batch, seq):
    tokens = x2d.shape[0]
    nq = seq // TQ
    tok_spec = lambda width: pl.BlockSpec((TQ, width), lambda b, j: (b * nq + j, 0))
    return pl.pallas_call(
        _attn_kernel,
        grid=(batch, nq),
        in_specs=[
            tok_spec(N_HEADS * HEAD_PAD),
            pl.BlockSpec((pl.Squeezed(), N_HEADS * HEAD_PAD, seq), lambda b, j: (b, 0, 0)),
            pl.BlockSpec((seq, MLA_WIDTH), lambda b, j: (b, 0)),
            tok_spec(D_MODEL), tok_spec(D_MODEL), tok_spec(D_MODEL),
            pl.BlockSpec((pl.Squeezed(), N_MOD_ROWS, D_MODEL), lambda b, j: (b, 0, 0)),
            _const_spec((MLA_WIDTH, D_MODEL)), _const_spec((D_MODEL, D_MODEL)),
        ],
        out_specs=tok_spec(D_MODEL),
        out_shape=jax.ShapeDtypeStruct((tokens, D_MODEL), F32),
        scratch_shapes=[pltpu.VMEM((S_SLOTS, TK, seq), F32), pltpu.VMEM((TQ, MLA_WIDTH), BF16)],
        compiler_params=pltpu.CompilerParams(
            dimension_semantics=("arbitrary", "arbitrary"), vmem_limit_bytes=VMEM_LIMIT),
        name="attn",
    )(q, kt, v, p, g, x2d, mod, w_mla, w_out)


def _prep_mixer_weights(w_in, pool_grp, pool_scale, w_pool_proj, q_a_norm, w_q_up, kv_a_norm, w_kv_up,
                        q_norm_nope, q_norm_rope, k_norm_nope, k_norm_rope):
    splits = np.cumsum([POOL_WIDTH, Q_LORA_RANK, KV_LORA_RANK, QK_ROPE_DIM, D_MODEL])
    u_w, q_w, kv_w, kr_w, gp_w, gm_w = jnp.split(w_in, splits, axis=1)
    kr_w = jnp.pad(kr_w, ((0, 0), (QK_NOPE_DIM, HEAD_PAD - QK_NOPE_DIM - QK_ROPE_DIM)))
    w_in_p = jnp.concatenate([u_w, q_w, kr_w, kv_w, gp_w, gm_w], axis=1).astype(BF16)

    qk_head = QK_NOPE_DIM + QK_ROPE_DIM
    w_q = jnp.pad(w_q_up.reshape(Q_LORA_RANK, N_HEADS, qk_head), ((0, 0), (0, 0), (0, HEAD_PAD - qk_head)))
    w_q = w_q.reshape(Q_LORA_RANK, N_HEADS * HEAD_PAD).astype(BF16)
    kv = w_kv_up.reshape(KV_LORA_RANK, N_HEADS, QK_NOPE_DIM + V_HEAD_DIM)
    w_k = jnp.pad(kv[..., :QK_NOPE_DIM], ((0, 0), (0, 0), (0, HEAD_PAD - QK_NOPE_DIM)))
    w_kv = jnp.concatenate([w_k.reshape(KV_LORA_RANK, N_HEADS * HEAD_PAD),
                            kv[..., QK_NOPE_DIM:].reshape(KV_LORA_RANK, MLA_WIDTH)], axis=1).astype(BF16)

    zeros = lambda n: jnp.zeros((n,), F32)
    q_gain = jnp.tile(jnp.concatenate([q_norm_nope, q_norm_rope, zeros(HEAD_PAD - qk_head)]), N_HEADS)
    k_gain = jnp.tile(jnp.concatenate([k_norm_nope, zeros(HEAD_PAD - QK_NOPE_DIM)]), N_HEADS)
    kr_gain = jnp.concatenate([zeros(QK_NOPE_DIM), k_norm_rope, zeros(HEAD_PAD - qk_head)])

    seg = np.zeros((2 * HEAD_PAD, 2 * HEAD_PAD), np.float32)
    for base in (0, HEAD_PAD):
        seg[base:base + QK_NOPE_DIM, base:base + QK_NOPE_DIM] = 1.0 / QK_NOPE_DIM
        seg[base + QK_NOPE_DIM:base + qk_head, base + QK_NOPE_DIM:base + qk_head] = 1.0 / QK_ROPE_DIM

    inv_freq = ROPE_THETA ** (-jnp.arange(0, QK_ROPE_DIM, 2, dtype=F32) / QK_ROPE_DIM)
    freq = jnp.concatenate([zeros(QK_NOPE_DIM), inv_freq, inv_freq, zeros(HEAD_PAD - qk_head)])

    return {
        "w_in": w_in_p, "pool_grp": pool_grp.astype(BF16), "pool_scale": pool_scale.reshape(1, POOL_WIDTH),
        "w_pool_proj": w_pool_proj.astype(BF16), "q_a_norm": q_a_norm.reshape(1, Q_LORA_RANK), "w_q": w_q,
        "kv_a_norm": kv_a_norm.reshape(1, KV_LORA_RANK), "w_kv": w_kv,
        "q_gain": q_gain.reshape(1, -1), "k_gain": k_gain.reshape(1, -1), "kr_gain": kr_gain.reshape(1, -1),
        "seg": jnp.asarray(seg, BF16), "freq": freq.reshape(1, HEAD_PAD),
    }


def kernel(x, c, positions, w_ada, b_ada, norm_ffn1, w_ffn1_in, w_ffn1_out, norm_mix, w_in, pool_grp,
           pool_scale, w_pool_proj, q_a_norm, w_q_up, kv_a_norm, w_kv_up, q_norm_nope, q_norm_rope,
           k_norm_nope, k_norm_rope, w_mla_proj, w_out, norm_ffn2, w_ffn2_in, w_ffn2_out):
    batch, seq, d = x.shape
    depth = w_ada.shape[0]
    assert d == D_MODEL and seq % TM_FFN == 0 and seq % TM_MIX == 0
    assert seq % TQ == 0 and TQ % TK == 0
    x2d = x.reshape(batch * seq, d)
    pos2d = positions.reshape(batch * seq // HEAD_PAD, HEAD_PAD)
    for l in range(depth):
        mod = _ada(c, w_ada[l], b_ada[l]).reshape(batch, N_MOD_ROWS, d)
        x2d = _ffn(x2d, mod, norm_ffn1[l].reshape(1, d), w_ffn1_in[l].astype(BF16),
                   w_ffn1_out[l].astype(BF16), sub=0, seq=seq)
        w = _prep_mixer_weights(w_in[l], pool_grp[l], pool_scale[l], w_pool_proj[l], q_a_norm[l], w_q_up[l],
                                kv_a_norm[l], w_kv_up[l], q_norm_nope[l], q_norm_rope[l], k_norm_nope[l],
                                k_norm_rope[l])
        q, kt, v, p, g = _mix_in(x2d, mod, pos2d, norm_mix[l].reshape(1, d), w, seq=seq)
        x2d = _attn(q, kt, v, p, g, x2d, mod, w_mla_proj[l].astype(BF16), w_out[l].astype(BF16),
                    batch=batch, seq=seq)
        x2d = _ffn(x2d, mod, norm_ffn2[l].reshape(1, d), w_ffn2_in[l].astype(BF16),
                   w_ffn2_out[l].astype(BF16), sub=2, seq=seq)
    return x2d.reshape(batch, seq, d)
```
